```python
import jax, jax.numpy as jnp
from jax import lax
import numpy as np

D_MODEL = 1024
BATCH = 4
SEQ = 4096
DEPTH = 2
DEC_BATCH = 32
DEC_SEQ = 1
PAST_LEN = 16384
PAGE_SIZE = 128

MIX_HALF = D_MODEL // 2
SC_WIDTH = MIX_HALF
SC_K = 3
GLA_HEADS = 4
GLA_DV = MIX_HALF // GLA_HEADS
GLA_DK = GLA_DV // 2
GLA_RANK = 16
GLA_TAU = 16.0
GLA_CHUNK = 16
FOX_HEADS = 8
FOX_HD = MIX_HALF // FOX_HEADS
Q_BLOCK = 128
FOX_BIAS_LO = 3.0
FOX_BIAS_HI = 12.0
CF_WIDTH = MIX_HALF
CF_K = 31
D_FF = ((8 * D_MODEL + 3 * 256 - 1) // (3 * 256)) * 256
DN_ALPHA = (2 * DEPTH) ** 0.25
DN_BETA = (8 * DEPTH) ** -0.25
EPS = 1e-5
AB_SPLITS = (SC_WIDTH, SC_WIDTH, SC_WIDTH, GLA_HEADS * GLA_DK, GLA_HEADS * GLA_DK, GLA_HEADS * GLA_DV, GLA_HEADS * GLA_DV, GLA_RANK)
CD_SPLITS = (FOX_HEADS * FOX_HD, FOX_HEADS * FOX_HD, FOX_HEADS * FOX_HD, FOX_HEADS, CF_WIDTH, CF_WIDTH)
F32 = jnp.float32

kernel_name = 'hybrid_conv_gla_fox_conformer_step'


def _split(t, sizes):
    idx = [int(i) for i in np.cumsum(sizes)[:-1]]
    return jnp.split(t, idx, axis=-1)


def _layernorm(x, g, b):
    xf = x.astype(F32)
    mu = jnp.mean(xf, axis=-1, keepdims=True)
    var = jnp.mean(jnp.square(xf - mu), axis=-1, keepdims=True)
    return ((xf - mu) * lax.rsqrt(var + EPS) * g + b).astype(x.dtype)


def _rmsnorm(x, g):
    xf = x.astype(F32)
    return (xf * lax.rsqrt(jnp.mean(xf * xf, axis=-1, keepdims=True) + EPS) * g).astype(x.dtype)


def _causal_dwconv(u, prev, w):
    ext = jnp.concatenate([prev.astype(u.dtype), u], axis=1)
    y = lax.conv_general_dilated(ext, w[:, None, :].astype(u.dtype), window_strides=(1,), padding='VALID',
                                 dimension_numbers=('NWC', 'WIO', 'NWC'), feature_group_count=u.shape[-1])
    return y, ext[:, -(w.shape[0] - 1):]


def _gla(q, k, v, log_a, s0):
    b_, l_, h_, _ = q.shape
    dv = v.shape[-1]
    c = min(GLA_CHUNK, l_)
    n = -(-l_ // c)
    pad = ((0, 0), (0, n * c - l_), (0, 0), (0, 0))
    q, k, v, log_a = (jnp.pad(t, pad) for t in (q, k, v, log_a))

    def chunks(t):
        return t.reshape(b_, n, c, h_, t.shape[-1]).transpose(1, 0, 3, 2, 4)

    qc, kc, vc = chunks(q), chunks(k), chunks(v)
    g = jnp.cumsum(chunks(log_a).astype(F32), axis=3)
    g_last = g[:, :, :, -1:, :]
    causal = jnp.tril(jnp.ones((c, c), bool))
    rel = g[:, :, :, :, None, :] - g[:, :, :, None, :, :]
    decay = jnp.exp(jnp.where(causal[:, :, None], rel, -jnp.inf))
    scores = jnp.einsum('nbhtd,nbhsd,nbhtsd->nbhts', qc, kc, decay)
    o_intra = jnp.einsum('nbhts,nbhsv->nbhtv', scores, vc)
    q_read = qc * jnp.exp(g)
    k_write = kc * jnp.exp(g_last - g)
    a_chunk = jnp.exp(g_last[:, :, :, 0, :])

    def step(s, xs):
        qr, kw, vv, ac = xs
        o = jnp.einsum('bhtd,bhdv->bhtv', qr, s)
        s = ac[..., None] * s + jnp.einsum('bhtd,bhtv->bhdv', kw, vv)
        return s, o

    s_fin, o_inter = lax.scan(step, s0.astype(F32), (q_read, k_write, vc, a_chunk))
    o = (o_intra + o_inter).transpose(1, 0, 3, 2, 4).reshape(b_, n * c, h_, dv)[:, :l_]
    return o.astype(v.dtype), s_fin.astype(s0.dtype)


def _fox_cum(lf):
    lf = lf.astype(F32).transpose(0, 2, 1)
    return lf - lax.cumsum(lf, axis=2, reverse=True)


def _fox_logits(q, k, cq, ck, qi, ki):
    s = jnp.einsum('bqhd,bkhd->bhqk', q, k, preferred_element_type=F32) * (FOX_HD ** -0.5)
    s = s + cq[..., :, None] - ck[..., None, :]
    return jnp.where(ki[None, :] <= qi[:, None], s, -jnp.inf)


def _fox_prompt(q, k, v, lf):
    b_, l_, h_, hd = q.shape
    c = _fox_cum(lf)
    ki = jnp.arange(l_)
    nb = l_ // Q_BLOCK
    qb = q.reshape(b_, nb, Q_BLOCK, h_, hd).transpose(1, 0, 2, 3, 4)
    cb = c.reshape(b_, h_, nb, Q_BLOCK).transpose(2, 0, 1, 3)
    ib = ki.reshape(nb, Q_BLOCK)

    def block(args):
        qq, cc, ii = args
        p = jax.nn.softmax(_fox_logits(qq, k, cc, c, ii, ki), axis=-1)
        return jnp.einsum('bhqk,bkhd->bqhd', p.astype(v.dtype), v)

    o = lax.map(block, (qb, cb, ib))
    return o.transpose(1, 0, 2, 3, 4).reshape(b_, l_, h_, hd)


def _fox_sample(q, k, v, lf, k_past, v_past, lf_past):
    l_ = q.shape[1]
    p_len = k_past.shape[1]
    c = _fox_cum(jnp.concatenate([lf_past.astype(F32), lf], axis=1))
    c_past, c_new = c[..., :p_len], c[..., p_len:]
    qi = p_len + jnp.arange(l_)
    s_past = _fox_logits(q, k_past, c_new, c_past, qi, jnp.arange(p_len))
    s_new = _fox_logits(q, k, c_new, c_new, qi, qi)
    p = jax.nn.softmax(jnp.concatenate([s_past, s_new], axis=-1), axis=-1).astype(v.dtype)
    return (jnp.einsum('bhqk,bkhd->bqhd', p[..., :p_len], v_past)
            + jnp.einsum('bhqk,bkhd->bqhd', p[..., p_len:], v))


def _mixer_ab(x, conv_prev, gla_s0, p):
    w_in, conv_w, w_gate2, b_gate, norm_g, w_out = p
    b_, l_, _ = x.shape
    bg, cg, hv, q, k, v, g, r = _split(x @ w_in, AB_SPLITS)
    y_a, conv_state = _causal_dwconv(cg * hv, conv_prev, conv_w)
    y_a = bg * y_a
    log_a = jax.nn.log_sigmoid((r @ w_gate2 + b_gate).astype(F32)) / GLA_TAU
    hk = lambda t: t.reshape(b_, l_, GLA_HEADS, GLA_DK)
    o, gla_state = _gla(hk(q) * (GLA_DK ** -0.5), hk(k), v.reshape(b_, l_, GLA_HEADS, GLA_DV), hk(log_a), gla_s0)
    o = _rmsnorm(o, norm_g).reshape(b_, l_, -1) * jax.nn.silu(g)
    return jnp.concatenate([y_a, o], axis=-1) @ w_out, conv_state, gla_state


def _mixer_cd(x, past, conv_prev, p):
    w_in, b_f, conv_w, conv_b, ln_g, ln_b, w_out = p
    b_, l_, _ = x.shape
    q, k, v, f, a, gate = _split(x @ w_in, CD_SPLITS)
    hs = lambda t: t.reshape(b_, l_, FOX_HEADS, FOX_HD)
    q, k, v = hs(q), hs(k), hs(v)
    logf = jax.nn.log_sigmoid((f + b_f).astype(F32))
    if past is None:
        o_c = _fox_prompt(q, k, v, logf)
    else:
        o_c = _fox_sample(q, k, v, logf, *past)
    u = a * jax.nn.sigmoid(gate)
    y_d, conv_state = _causal_dwconv(u, conv_prev, conv_w)
    y_d = jax.nn.silu(_layernorm(y_d + conv_b, ln_g, ln_b))
    out = jnp.concatenate([o_c.reshape(b_, l_, -1), y_d], axis=-1) @ w_out
    return out, k, v, logf.astype(x.dtype), conv_state


def _ffn(x, w_in, w_out):
    gate, up = jnp.split(x @ w_in, 2, axis=-1)
    return (jax.nn.silu(gate) * up) @ w_out


def _trunk(x, conv_a, gla_s, past, conv_d, ab_p, cd_p, ffn_w_in, ffn_w_out, ln_g, ln_b):
    for layer in range(DEPTH):
        if layer % 2 == 0:
            m, conv_a, gla_s = _mixer_ab(x, conv_a, gla_s, ab_p)
        else:
            m, k_rows, v_rows, lf_rows, conv_d = _mixer_cd(x, past, conv_d, cd_p)
        x = _layernorm(DN_ALPHA * x + m, ln_g[layer, 0], ln_b[layer, 0])
        x = _layernorm(DN_ALPHA * x + _ffn(x, ffn_w_in[layer], ffn_w_out[layer]), ln_g[layer, 1], ln_b[layer, 1])
    return x, conv_a, gla_s, k_rows, v_rows, lf_rows, conv_d


def setup_inputs(seed: int = 0) -> dict:
    key = jax.random.key(seed)
    ks = iter(jax.random.split(key, 32))

    def nrm(shape, scale):
        return jax.random.normal(next(ks), shape, F32) * scale

    n_pages = PAST_LEN // PAGE_SIZE
    used = DEC_BATCH * n_pages
    n_phys = used + max(1, used // 4)
    page_table = jax.random.permutation(next(ks), n_phys)[:used].reshape(DEC_BATCH, n_pages).astype(jnp.int32)
    fox_b_f = jnp.linspace(FOX_BIAS_LO, FOX_BIAS_HI, FOX_HEADS, dtype=F32) + nrm((FOX_HEADS,), 0.1)
    return {
        'x_prompt': nrm((BATCH, SEQ, D_MODEL), 1.0),
        'x_sample': nrm((DEC_BATCH, DEC_SEQ, D_MODEL), 1.0),
        'state_conv_a': nrm((DEC_BATCH, SC_K - 1, SC_WIDTH), 1.0),
        'state_gla': nrm((DEC_BATCH, GLA_HEADS, GLA_DK, GLA_DV), 1.0),
        'cache_k': nrm((n_phys, PAGE_SIZE, FOX_HEADS, FOX_HD), 1.0),
        'cache_v': nrm((n_phys, PAGE_SIZE, FOX_HEADS, FOX_HD), 1.0),
        'cache_logf': jax.nn.log_sigmoid(fox_b_f + nrm((n_phys, PAGE_SIZE, FOX_HEADS), 1.0)),
        'state_conv_d': nrm((DEC_BATCH, CF_K - 1, CF_WIDTH), 0.5),
        'page_table': page_table,
        'ab_w_in': nrm((D_MODEL, sum(AB_SPLITS)), D_MODEL ** -0.5),
        'ab_conv_w': nrm((SC_K, SC_WIDTH), SC_K ** -0.5),
        'gla_w_gate2': nrm((GLA_RANK, GLA_HEADS * GLA_DK), GLA_RANK ** -0.5),
        'gla_b_gate': nrm((GLA_HEADS * GLA_DK,), 0.01),
        'gla_norm_g': 1.0 + nrm((GLA_DV,), 0.01),
        'ab_w_out': nrm((D_MODEL, D_MODEL), DN_BETA * D_MODEL ** -0.5),
        'cd_w_in': nrm((D_MODEL, sum(CD_SPLITS)), D_MODEL ** -0.5),
        'fox_b_f': fox_b_f,
        'cf_conv_w': nrm((CF_K, CF_WIDTH), CF_K ** -0.5),
        'cf_conv_b': nrm((CF_WIDTH,), 0.01),
        'cf_ln_g': 1.0 + nrm((CF_WIDTH,), 0.01),
        'cf_ln_b': nrm((CF_WIDTH,), 0.01),
        'cd_w_out': nrm((D_MODEL, D_MODEL), DN_BETA * D_MODEL ** -0.5),
        'ffn_w_in': nrm((DEPTH, D_MODEL, 2 * D_FF), D_MODEL ** -0.5),
        'ffn_w_out': nrm((DEPTH, D_FF, D_MODEL), DN_BETA * D_FF ** -0.5),
        'ln_g': 1.0 + nrm((DEPTH, 2, D_MODEL), 0.01),
        'ln_b': nrm((DEPTH, 2, D_MODEL), 0.01),
    }


def reference(x_prompt, x_sample, state_conv_a, state_gla, cache_k, cache_v, cache_logf, state_conv_d, page_table,
              ab_w_in, ab_conv_w, gla_w_gate2, gla_b_gate, gla_norm_g, ab_w_out,
              cd_w_in, fox_b_f, cf_conv_w, cf_conv_b, cf_ln_g, cf_ln_b, cd_w_out,
              ffn_w_in, ffn_w_out, ln_g, ln_b):
    ab_p = (ab_w_in, ab_conv_w, gla_w_gate2, gla_b_gate, gla_norm_g, ab_w_out)
    cd_p = (cd_w_in, fox_b_f, cf_conv_w, cf_conv_b, cf_ln_g, cf_ln_b, cd_w_out)
    nb, dt = x_prompt.shape[0], x_prompt.dtype
    y_prompt, conv_a_p, gla_p, k_p, v_p, lf_p, conv_d_p = _trunk(
        x_prompt, jnp.zeros((nb, SC_K - 1, SC_WIDTH), dt), jnp.zeros((nb, GLA_HEADS, GLA_DK, GLA_DV), dt),
        None, jnp.zeros((nb, CF_K - 1, CF_WIDTH), dt), ab_p, cd_p, ffn_w_in, ffn_w_out, ln_g, ln_b)
    nd, n_pages = page_table.shape
    p_len = n_pages * PAGE_SIZE
    k_past = cache_k[page_table].reshape(nd, p_len, FOX_HEADS, FOX_HD)
    v_past = cache_v[page_table].reshape(nd, p_len, FOX_HEADS, FOX_HD)
    lf_past = cache_logf[page_table].reshape(nd, p_len, FOX_HEADS)
    y_sample, conv_a_s, gla_s, k_s, v_s, lf_s, conv_d_s = _trunk(
        x_sample, state_conv_a, state_gla, (k_past, v_past, lf_past), state_conv_d,
        ab_p, cd_p, ffn_w_in, ffn_w_out, ln_g, ln_b)
    return (y_prompt, y_sample, conv_a_p, conv_a_s, gla_p, gla_s, k_p, k_s, v_p, v_s, lf_p, lf_s, conv_d_p, conv_d_s)
```

```python
import functools

import numpy as np
import jax
import jax.numpy as jnp
from jax import lax
from jax.experimental import pallas as pl
from jax.experimental.pallas import tpu as pltpu

F32 = jnp.float32
BF16 = jnp.bfloat16

D_MODEL = 1024
MIX_HALF = D_MODEL // 2
SC_K = 3
GLA_HEADS = 4
GLA_DV = MIX_HALF // GLA_HEADS
GLA_DK = GLA_DV // 2
GLA_RANK = 16
GLA_INV_TAU = 1.0 / 16.0
GLA_CHUNK = 128
GLA_LEVELS = 7
FOX_HEADS = 8
FOX_HD = MIX_HALF // FOX_HEADS
CF_K = 31
D_FF = 2816
DEPTH = 2
DN_ALPHA = (2 * DEPTH) ** 0.25
EPS = 1e-5
R_PAD = 128
F_PAD = 16
VMEM_LIMIT = 56 * 1024 * 1024
PAGES_PER_STEP = 16


def _cparams(*sem):
    return pltpu.CompilerParams(dimension_semantics=sem, vmem_limit_bytes=VMEM_LIMIT)


def _log_sigmoid(x):
    return jnp.minimum(x, 0.0) - jnp.log1p(jnp.exp(-jnp.abs(x)))


def _sigmoid(x):
    return 1.0 / (1.0 + jnp.exp(-x))


def _silu(x):
    return x * _sigmoid(x)


def _layernorm(y, g, b):
    mu = jnp.mean(y, axis=-1, keepdims=True)
    d = y - mu
    var = jnp.mean(d * d, axis=-1, keepdims=True)
    return d * lax.rsqrt(var + EPS) * g + b


def _dot(a, b):
    return jnp.dot(a, b, preferred_element_type=F32)


def _dot_nt(a, b):
    return lax.dot_general(a, b, (((1,), (1,)), ((), ())), preferred_element_type=F32)


def _proj_kernel(t_splits, x_ref, wn_ref, wt_ref, pn_ref, *pt_refs):
    xb = x_ref[...].astype(BF16)
    pn_ref[...] = _dot(xb, wn_ref[...])
    pt = _dot_nt(wt_ref[...], xb)
    r0 = 0
    for ref, n in zip(pt_refs, t_splits):
        ref[0] = pt[r0:r0 + n]
        r0 += n


def _proj(x2d, wn, wt, t_splits, nb, tl):
    m, k = x2d.shape
    l = m // nb
    nl = l // tl
    nn = wn.shape[1]
    nt = wt.shape[0]
    out_shape = [jax.ShapeDtypeStruct((m, nn), F32)] + [jax.ShapeDtypeStruct((nb, n, l), F32) for n in t_splits]
    out_specs = [pl.BlockSpec((tl, nn), lambda b, i: (b * nl + i, 0))] + [
        pl.BlockSpec((1, n, tl), lambda b, i: (b, 0, i)) for n in t_splits]
    return pl.pallas_call(
        functools.partial(_proj_kernel, t_splits),
        out_shape=out_shape,
        grid=(nb, nl),
        in_specs=[pl.BlockSpec((tl, k), lambda b, i: (b * nl + i, 0)),
                  pl.BlockSpec((k, nn), lambda b, i: (0, 0)),
                  pl.BlockSpec((nt, k), lambda b, i: (0, 0))],
        out_specs=out_specs,
        compiler_params=_cparams("parallel", "arbitrary"),
        name="proj",
    )(x2d, wn, wt)


def _mm_res_ln_kernel(h1_ref, h2_ref, w_ref, x_ref, g_ref, b_ref, o_ref):
    half = h1_ref.shape[1]
    m = _dot(h1_ref[...], w_ref[0:half, :]) + _dot(h2_ref[...], w_ref[half:2 * half, :])
    o_ref[...] = _layernorm(DN_ALPHA * x_ref[...] + m, g_ref[...], b_ref[...])


def _mm_res_ln(h1, h1_col, h2, h2_col, w, x2d, g, b, tm):
    m, d = x2d.shape
    half = d // 2
    return pl.pallas_call(
        _mm_res_ln_kernel,
        out_shape=jax.ShapeDtypeStruct((m, d), F32),
        grid=(m // tm,),
        in_specs=[pl.BlockSpec((tm, half), lambda i: (i, h1_col)),
                  pl.BlockSpec((tm, half), lambda i: (i, h2_col)),
                  pl.BlockSpec((d, d), lambda i: (0, 0)),
                  pl.BlockSpec((tm, d), lambda i: (i, 0)),
                  pl.BlockSpec((1, d), lambda i: (0, 0)),
                  pl.BlockSpec((1, d), lambda i: (0, 0))],
        out_specs=pl.BlockSpec((tm, d), lambda i: (i, 0)),
        compiler_params=_cparams("parallel"),
        name="mm_res_ln",
    )(h1, h2, w, x2d, g, b)


def _ffn_kernel(x_ref, wg_ref, wu_ref, wo_ref, g_ref, b_ref, o_ref, acc_ref, xb_ref):
    j = pl.program_id(1)

    @pl.when(j == 0)
    def _():
        acc_ref[...] = jnp.zeros_like(acc_ref)
        xb_ref[...] = x_ref[...].astype(BF16)

    xb = xb_ref[...]
    gate = _dot(xb, wg_ref[...])
    up = _dot(xb, wu_ref[...])
    h = (_silu(gate) * up).astype(BF16)
    acc_ref[...] += _dot(h, wo_ref[...])

    @pl.when(j == pl.num_programs(1) - 1)
    def _():
        o_ref[...] = _layernorm(DN_ALPHA * x_ref[...] + acc_ref[...], g_ref[...], b_ref[...])


def _ffn(x2d, w_in, w_out, g, b, tm, tf):
    m, d = x2d.shape
    dff = w_out.shape[0]
    nf = dff // tf
    return pl.pallas_call(
        _ffn_kernel,
        out_shape=jax.ShapeDtypeStruct((m, d), F32),
        grid=(m // tm, nf),
        in_specs=[pl.BlockSpec((tm, d), lambda i, j: (i, 0)),
                  pl.BlockSpec((d, tf), lambda i, j: (0, j)),
                  pl.BlockSpec((d, tf), lambda i, j: (0, j + nf)),
                  pl.BlockSpec((tf, d), lambda i, j: (j, 0)),
                  pl.BlockSpec((1, d), lambda i, j: (0, 0)),
                  pl.BlockSpec((1, d), lambda i, j: (0, 0))],
        out_specs=pl.BlockSpec((tm, d), lambda i, j: (i, 0)),
        scratch_shapes=[pltpu.VMEM((tm, d), F32), pltpu.VMEM((tm, d), BF16)],
        compiler_params=_cparams("parallel", "arbitrary"),
        name="ffn",
    )(x2d, w_in, w_in, w_out, g, b)


def _gla_sum_matrices():
    c = GLA_CHUNK
    t = np.arange(c)[:, None]
    u = np.arange(c)[None, :]
    mq = [(u <= t)]
    mk = [(u > t)]
    for l in range(1, GLA_LEVELS + 1):
        blk, half = 1 << l, 1 << (l - 1)
        same = (t // blk) == (u // blk)
        mq.append(same & (t % blk >= half) & (u % blk >= half) & (u <= t))
        mk.append(same & (t % blk < half) & (u % blk < half) & (u > t))
    mk.append(np.ones((c, c), bool))
    mq = np.concatenate(mq, axis=0).astype(np.float32)
    mk = np.concatenate(mk, axis=0).astype(np.float32).T
    mq2 = np.concatenate([mq, mq], axis=1)
    mk2 = np.concatenate([mk, mk], axis=0)
    return jnp.asarray(mq2, BF16), jnp.asarray(mk2, BF16)


def _split_hi_lo(x):
    hi = x.astype(BF16)
    lo = (x - hi.astype(F32)).astype(BF16)
    return hi, lo


def _mix0_kernel(bg_ref, cg_ref, hv_ref, v_ref, gt_ref, q_ref, r_ref, kt_ref, rt_ref,
                 convw_ref, w2_ref, w2t_ref, bg2_ref, bg2c_ref, ng_ref, mq_ref, mk_ref,
                 mixed_ref, convst_ref, glast_ref, s_ref, ext_ref):
    c = GLA_CHUNK
    i = pl.program_id(1)

    @pl.when(i == 0)
    def _():
        s_ref[...] = jnp.zeros_like(s_ref)
        ext_ref[0:8, :] = jnp.zeros((8, MIX_HALF), F32)

    u = cg_ref[...] * hv_ref[...]
    ext_ref[8:8 + c, :] = u
    w = convw_ref[...]
    y = w[2:3, :] * u + w[1:2, :] * ext_ref[7:7 + c, :] + w[0:1, :] * ext_ref[6:6 + c, :]
    mixed_ref[:, 0:MIX_HALF] = (bg_ref[...] * y).astype(BF16)
    ext_ref[0:8, :] = u[c - 8:c, :]
    convst_ref[0] = u[c - (SC_K - 1):c, :]

    z = _dot(r_ref[...].astype(BF16), w2_ref[...]) + bg2_ref[...]
    la = _log_sigmoid(z) * GLA_INV_TAU
    zt = _dot(w2t_ref[...], rt_ref[0].astype(BF16)) + bg2c_ref[...]
    lat = _log_sigmoid(zt) * GLA_INV_TAU
    hi, lo = _split_hi_lo(la)
    eq = _dot(mq_ref[...], jnp.concatenate([hi, lo], axis=0))
    hit, lot = _split_hi_lo(lat)
    ek = _dot(jnp.concatenate([hit, lot], axis=1), mk_ref[...])

    q = q_ref[...] * (GLA_DK ** -0.5)
    kt = kt_ref[0]
    tt = lax.broadcasted_iota(jnp.int32, (c, c), 0)
    ss = lax.broadcasted_iota(jnp.int32, (c, c), 1)
    xr = tt ^ ss
    lower = tt > ss
    for h in range(GLA_HEADS):
        ks = slice(h * GLA_DK, (h + 1) * GLA_DK)
        vs = slice(h * GLA_DV, (h + 1) * GLA_DV)
        qh = q[:, ks]
        kth = kt[ks, :]
        vh = v_ref[:, vs].astype(BF16)
        a = jnp.where(xr == 0, _dot(qh.astype(BF16), kth.astype(BF16)), 0.0)
        for l in range(1, GLA_LEVELS + 1):
            ql = (qh * jnp.exp(eq[l * c:(l + 1) * c, ks])).astype(BF16)
            kl = (kth * jnp.exp(ek[ks, l * c:(l + 1) * c])).astype(BF16)
            lvl = lower & (xr >= (1 << (l - 1))) & (xr < (1 << l))
            a = a + jnp.where(lvl, _dot(ql, kl), 0.0)
        s_old = s_ref[h]
        q_read = (qh * jnp.exp(eq[0:c, ks])).astype(BF16)
        o = _dot(a.astype(BF16), vh) + _dot(q_read, s_old.astype(BF16))
        k_write = (kth * jnp.exp(ek[ks, 0:c])).astype(BF16)
        a_chunk = jnp.exp(ek[ks, (GLA_LEVELS + 1) * c:(GLA_LEVELS + 2) * c])
        s_new = a_chunk * s_old + _dot(k_write, vh)
        s_ref[h] = s_new
        glast_ref[0, h] = s_new
        on = o * lax.rsqrt(jnp.mean(o * o, axis=-1, keepdims=True) + EPS) * ng_ref[...]
        mixed_ref[:, MIX_HALF + h * GLA_DV:MIX_HALF + (h + 1) * GLA_DV] = (on * _silu(gt_ref[:, vs])).astype(BF16)


def _mix0(pn, kt, rt, conv_w, w2, w2t, bg2, bg2c, ng, nb):
    m = pn.shape[0]
    l = m // nb
    c = GLA_CHUNK
    nl = l // c
    mq, mk = _gla_sum_matrices()
    row = lambda b, i: b * nl + i
    full = lambda a: pl.BlockSpec(a.shape, lambda b, i: (0,) * a.ndim)
    wide = lambda col: pl.BlockSpec((c, MIX_HALF), lambda b, i: (row(b, i), col))
    q_col = 5 * MIX_HALF // (GLA_HEADS * GLA_DK)
    r_col = (5 * MIX_HALF + GLA_HEADS * GLA_DK) // R_PAD
    return pl.pallas_call(
        _mix0_kernel,
        out_shape=[jax.ShapeDtypeStruct((m, D_MODEL), BF16),
                   jax.ShapeDtypeStruct((nb, SC_K - 1, MIX_HALF), F32),
                   jax.ShapeDtypeStruct((nb, GLA_HEADS, GLA_DK, GLA_DV), F32)],
        grid=(nb, nl),
        in_specs=[wide(0), wide(1), wide(2), wide(3), wide(4),
                  pl.BlockSpec((c, GLA_HEADS * GLA_DK), lambda b, i: (row(b, i), q_col)),
                  pl.BlockSpec((c, R_PAD), lambda b, i: (row(b, i), r_col)),
                  pl.BlockSpec((1, GLA_HEADS * GLA_DK, c), lambda b, i: (b, 0, i)),
                  pl.BlockSpec((1, GLA_RANK, c), lambda b, i: (b, 0, i)),
                  full(conv_w), full(w2), full(w2t), full(bg2), full(bg2c), full(ng), full(mq), full(mk)],
        out_specs=[pl.BlockSpec((c, D_MODEL), lambda b, i: (row(b, i), 0)),
                   pl.BlockSpec((1, SC_K - 1, MIX_HALF), lambda b, i: (b, 0, 0)),
                   pl.BlockSpec((1, GLA_HEADS, GLA_DK, GLA_DV), lambda b, i: (b, 0, 0, 0))],
        scratch_shapes=[pltpu.VMEM((GLA_HEADS, GLA_DK, GLA_DV), F32), pltpu.VMEM((c + 8, MIX_HALF), F32)],
        compiler_params=_cparams("parallel", "arbitrary"),
        name="mix0",
    )(pn, pn, pn, pn, pn, pn, pn, kt, rt, conv_w, w2, w2t, bg2, bg2c, ng, mq, mk)


def _fox_c_kernel(ft_ref, bf_ref, lf_ref, c_ref):
    lf = _log_sigmoid(ft_ref[0] + bf_ref[...])
    lf_ref[0] = lf
    n = lf.shape[1]
    lane = lax.broadcasted_iota(jnp.int32, lf.shape, 1)
    suf = lf
    s = 1
    while s < n:
        suf = suf + jnp.where(lane + s < n, pltpu.roll(suf, n - s, 1), 0.0)
        s *= 2
    c_ref[0] = lf - suf


def _fox_c(ft, bf_col):
    nb, _, l = ft.shape
    spec = pl.BlockSpec((1, FOX_HEADS, l), lambda b: (b, 0, 0))
    return pl.pallas_call(
        _fox_c_kernel,
        out_shape=[jax.ShapeDtypeStruct((nb, FOX_HEADS, l), F32)] * 2,
        grid=(nb,),
        in_specs=[spec, pl.BlockSpec((FOX_HEADS, 1), lambda b: (0, 0))],
        out_specs=[spec, spec],
        compiler_params=_cparams("parallel"),
        name="fox_c",
    )(ft, bf_col)


def _fox_flash_kernel(q_ref, kt_ref, vt_ref, c_ref, o_ref, m_ref, l_ref, acc_ref):
    qi = pl.program_id(1)
    kj = pl.program_id(2)
    tq = q_ref.shape[0]
    tk = kt_ref.shape[2]

    @pl.when(kj == 0)
    def _():
        m_ref[...] = jnp.full_like(m_ref, -jnp.inf)
        l_ref[...] = jnp.zeros_like(l_ref)
        acc_ref[...] = jnp.zeros_like(acc_ref)

    @pl.when(kj <= qi)
    def _():
        qpos = qi * tq + lax.broadcasted_iota(jnp.int32, (tq, tk), 0)
        kpos = kj * tk + lax.broadcasted_iota(jnp.int32, (tq, tk), 1)
        visible = kpos <= qpos
        first_half = lax.broadcasted_iota(jnp.int32, (tq, 2 * FOX_HD), 1) < FOX_HD
        for p in range(FOX_HEADS // 2):
            ps = slice(p * 2 * FOX_HD, (p + 1) * 2 * FOX_HD)
            qp = q_ref[:, ps] * (FOX_HD ** -0.5)
            ktp = kt_ref[0, ps, :].astype(BF16)
            vtp = vt_ref[0, ps, :].astype(BF16)
            outs, alphas = [], []
            for e in range(2):
                h = 2 * p + e
                qe = jnp.where(first_half if e == 0 else ~first_half, qp, 0.0).astype(BF16)
                s = _dot(qe, ktp) - c_ref[0, h:h + 1, :]
                s = jnp.where(visible, s, -jnp.inf)
                m_old = m_ref[h]
                m_new = jnp.maximum(m_old, jnp.max(s, axis=-1, keepdims=True))
                pr = jnp.exp(s - m_new)
                alpha = jnp.exp(m_old - m_new)
                l_ref[h] = alpha * l_ref[h] + jnp.sum(pr, axis=-1, keepdims=True)
                m_ref[h] = m_new
                outs.append(_dot_nt(pr.astype(BF16), vtp))
                alphas.append(alpha)
            acc_ref[:, ps] = (acc_ref[:, ps] * jnp.where(first_half, alphas[0], alphas[1])
                              + jnp.where(first_half, outs[0], outs[1]))

    @pl.when(kj == qi)
    def _():
        first_half = lax.broadcasted_iota(jnp.int32, (tq, 2 * FOX_HD), 1) < FOX_HD
        for p in range(FOX_HEADS // 2):
            ps = slice(p * 2 * FOX_HD, (p + 1) * 2 * FOX_HD)
            inv = jnp.where(first_half, 1.0 / l_ref[2 * p], 1.0 / l_ref[2 * p + 1])
            o_ref[:, ps] = (acc_ref[:, ps] * inv).astype(BF16)


def _fox_flash(pn, kt, vt, ct, nb, tq):
    m = pn.shape[0]
    l = m // nb
    nq = l // tq
    kv_spec = pl.BlockSpec((1, MIX_HALF, tq), lambda b, i, j: (b, 0, jnp.minimum(i, j)))
    return pl.pallas_call(
        _fox_flash_kernel,
        out_shape=jax.ShapeDtypeStruct((m, MIX_HALF), BF16),
        grid=(nb, nq, nq),
        in_specs=[pl.BlockSpec((tq, MIX_HALF), lambda b, i, j: (b * nq + i, 0)),
                  kv_spec, kv_spec,
                  pl.BlockSpec((1, FOX_HEADS, tq), lambda b, i, j: (b, 0, jnp.minimum(i, j)))],
        out_specs=pl.BlockSpec((tq, MIX_HALF), lambda b, i, j: (b * nq + i, 0)),
        scratch_shapes=[pltpu.VMEM((FOX_HEADS, tq, 1), F32), pltpu.VMEM((FOX_HEADS, tq, 1), F32),
                        pltpu.VMEM((tq, MIX_HALF), F32)],
        compiler_params=_cparams("parallel", "parallel", "arbitrary"),
        name="fox_flash",
    )(pn, kt, vt, ct)


CF_HALO = 32


def _conf_kernel(a_ref, gate_ref, w_ref, cb_ref, g_ref, b_ref, y_ref, st_ref, ext_ref):
    t = a_ref.shape[0]
    i = pl.program_id(1)

    @pl.when(i == 0)
    def _():
        ext_ref[0:CF_HALO, :] = jnp.zeros((CF_HALO, MIX_HALF), F32)

    u = a_ref[...] * _sigmoid(gate_ref[...])
    ext_ref[CF_HALO:CF_HALO + t, :] = u
    off = CF_HALO - (CF_K - 1)
    acc = jnp.zeros((t, MIX_HALF), F32) + cb_ref[...]
    for j in range(CF_K):
        acc = acc + w_ref[j:j + 1, :] * ext_ref[off + j:off + j + t, :]
    y_ref[...] = _silu(_layernorm(acc, g_ref[...], b_ref[...])).astype(BF16)
    st_ref[0] = ext_ref[t + off:t + CF_HALO, :]
    ext_ref[0:CF_HALO, :] = ext_ref[t:t + CF_HALO, :]


def _conf(pn, w, cb, g, b, nb, t):
    m = pn.shape[0]
    l = m // nb
    nl = l // t
    full = lambda a: pl.BlockSpec(a.shape, lambda bb, i: (0,) * a.ndim)
    return pl.pallas_call(
        _conf_kernel,
        out_shape=[jax.ShapeDtypeStruct((m, MIX_HALF), BF16),
                   jax.ShapeDtypeStruct((nb, CF_K - 1, MIX_HALF), F32)],
        grid=(nb, nl),
        in_specs=[pl.BlockSpec((t, MIX_HALF), lambda bb, i: (bb * nl + i, 1)),
                  pl.BlockSpec((t, MIX_HALF), lambda bb, i: (bb * nl + i, 2)),
                  full(w), full(cb), full(g), full(b)],
        out_specs=[pl.BlockSpec((t, MIX_HALF), lambda bb, i: (bb * nl + i, 0)),
                   pl.BlockSpec((1, CF_K - 1, MIX_HALF), lambda bb, i: (bb, 0, 0))],
        scratch_shapes=[pltpu.VMEM((t + CF_HALO, MIX_HALF), F32)],
        compiler_params=_cparams("parallel", "arbitrary"),
        name="conformer",
    )(pn, pn, w, cb, g, b)


def _smix0_kernel(pn_ref, kc_ref, qc_ref, rc_ref, prev_ref, s_ref, convw_ref, w2t_ref, bg2c_ref, ng_ref,
                  mixed_ref, convst_ref, sout_ref):
    pn = pn_ref[0]
    bg, cg, hv = pn[:, 0:MIX_HALF], pn[:, MIX_HALF:2 * MIX_HALF], pn[:, 2 * MIX_HALF:3 * MIX_HALF]
    v, gt = pn[:, 3 * MIX_HALF:4 * MIX_HALF], pn[:, 4 * MIX_HALF:5 * MIX_HALF]
    u = cg * hv
    prev = prev_ref[0]
    w = convw_ref[...]
    y = w[0:1, :] * prev[0:1, :] + w[1:2, :] * prev[1:2, :] + w[2:3, :] * u
    mixed_ref[0, :, 0:MIX_HALF] = (bg * y).astype(BF16)
    convst_ref[0, 0:1, :] = prev[1:2, :]
    convst_ref[0, 1:2, :] = u

    rc = rc_ref[0].astype(BF16).astype(F32)
    w2t = w2t_ref[...].astype(F32)
    zc = bg2c_ref[...]
    for j in range(GLA_RANK):
        zc = zc + w2t[:, j:j + 1] * rc[j:j + 1, :]
    ac = jnp.exp(_log_sigmoid(zc) * GLA_INV_TAU)
    qc = qc_ref[0] * (GLA_DK ** -0.5)
    kc = kc_ref[0]
    for h in range(GLA_HEADS):
        ks = slice(h * GLA_DK, (h + 1) * GLA_DK)
        vs = slice(h * GLA_DV, (h + 1) * GLA_DV)
        s_new = ac[ks, :] * s_ref[0, h] + kc[ks, :] * v[:, vs]
        sout_ref[0, h] = s_new
        o = jnp.sum(qc[ks, :] * s_new, axis=0, keepdims=True)
        on = o * lax.rsqrt(jnp.mean(o * o, axis=-1, keepdims=True) + EPS) * ng_ref[...]
        mixed_ref[0, :, MIX_HALF + h * GLA_DV:MIX_HALF + (h + 1) * GLA_DV] = (on * _silu(gt[:, vs])).astype(BF16)


def _smix0(pn3, kc, qc, rc, prev, s0, conv_w, w2t, bg2c, ng):
    nd, _, nn = pn3.shape
    per = lambda a: pl.BlockSpec((1,) + a.shape[1:], lambda b: (b,) + (0,) * (a.ndim - 1))
    full = lambda a: pl.BlockSpec(a.shape, lambda b: (0,) * a.ndim)
    out_shape = [jax.ShapeDtypeStruct((nd, 1, D_MODEL), BF16),
                 jax.ShapeDtypeStruct((nd, SC_K - 1, MIX_HALF), F32),
                 jax.ShapeDtypeStruct(s0.shape, F32)]
    return pl.pallas_call(
        _smix0_kernel,
        out_shape=out_shape,
        grid=(nd,),
        in_specs=[per(pn3), per(kc), per(qc), per(rc), per(prev), per(s0),
                  full(conv_w), full(w2t), full(bg2c), full(ng)],
        out_specs=[per(o) for o in out_shape],
        compiler_params=_cparams("parallel"),
        name="smix0",
    )(pn3, kc, qc, rc, prev, s0, conv_w, w2t, bg2c, ng)


def _sconf_kernel(pn_ref, prev_ref, w_ref, cb_ref, g_ref, b_ref, y_ref, st_ref):
    pn = pn_ref[0]
    a, gate = pn[:, MIX_HALF:2 * MIX_HALF], pn[:, 2 * MIX_HALF:3 * MIX_HALF]
    u = a * _sigmoid(gate)
    prev = prev_ref[0]
    w = w_ref[...]
    acc = jnp.sum(w[0:CF_K - 1, :] * prev, axis=0, keepdims=True) + w[CF_K - 1:CF_K, :] * u + cb_ref[...]
    y_ref[0] = _silu(_layernorm(acc, g_ref[...], b_ref[...])).astype(BF16)
    st_ref[0, 0:CF_K - 2, :] = prev_ref[0, 1:CF_K - 1, :]
    st_ref[0, CF_K - 2:CF_K - 1, :] = u


def _sconf(pn3, prev, w, cb, g, b):
    nd = pn3.shape[0]
    per = lambda a: pl.BlockSpec((1,) + a.shape[1:], lambda bb: (bb,) + (0,) * (a.ndim - 1))
    full = lambda a: pl.BlockSpec(a.shape, lambda bb: (0,) * a.ndim)
    out_shape = [jax.ShapeDtypeStruct((nd, 1, MIX_HALF), BF16), jax.ShapeDtypeStruct(prev.shape, F32)]
    return pl.pallas_call(
        _sconf_kernel,
        out_shape=out_shape,
        grid=(nd,),
        in_specs=[per(pn3), per(prev), full(w), full(cb), full(g), full(b)],
        out_specs=[per(o) for o in out_shape],
        compiler_params=_cparams("parallel"),
        name="sconf",
    )(pn3, prev, w, cb, g, b)


def _decode_kernel(pt_ref, qc_ref, kc_ref, vc_ref, fc_ref, bf_ref, *refs):
    g = PAGES_PER_STEP
    k_refs, v_refs, lf_refs = refs[0:g], refs[g:2 * g], refs[2 * g:3 * g]
    o_ref, lfo_ref, m_ref, l_ref, r_ref, acc_ref = refs[3 * g:]
    step = pl.program_id(1)
    rows = k_refs[0].shape[3]
    scale = FOX_HD ** -0.5
    qc = qc_ref[0] * scale
    lane = lax.broadcasted_iota(jnp.int32, (FOX_HEADS, rows), 1)
    head_row = lax.broadcasted_iota(jnp.int32, (FOX_HEADS, rows), 0)

    @pl.when(step == 0)
    def _():
        lf_new = _log_sigmoid(fc_ref[0] + bf_ref[...])
        lfo_ref[0] = lf_new
        r_ref[...] = lf_new
        l_ref[...] = jnp.ones_like(l_ref)
        col0 = lax.broadcasted_iota(jnp.int32, (FOX_HD, rows), 1) == 0
        head_col = lax.broadcasted_iota(jnp.int32, (FOX_HEADS, 1), 0)
        s_self = jnp.zeros((FOX_HEADS, 1), F32)
        for h in range(FOX_HEADS):
            hs = slice(h * FOX_HD, (h + 1) * FOX_HD)
            s_h = jnp.sum(qc[hs, :] * kc_ref[0, hs, :], axis=0, keepdims=True)
            s_self = jnp.where(head_col == h, s_h, s_self)
            acc_ref[h] = jnp.where(col0, vc_ref[0, hs, :], 0.0)
        m_ref[...] = s_self

    r_run = r_ref[...]
    s_pages = []
    for pg in range(g):
        lf = lf_refs[pg][0]
        pre = lf
        sh = 1
        while sh < rows:
            pre = pre + jnp.where(lane >= sh, pltpu.roll(pre, sh, 1), 0.0)
            sh *= 2
        tot = pre[:, rows - 1:rows]
        bias = r_run + (tot - pre)
        r_run = r_run + tot
        s = jnp.zeros((FOX_HEADS, rows), F32)
        for h in range(FOX_HEADS):
            hs = slice(h * FOX_HD, (h + 1) * FOX_HD)
            sh_ = jnp.sum(qc[hs, :] * k_refs[pg][0, h], axis=0, keepdims=True)
            s = jnp.where(head_row == h, sh_, s)
        s_pages.append(s + bias)
    r_ref[...] = r_run

    m_old = m_ref[...]
    m_new = m_old
    for s in s_pages:
        m_new = jnp.maximum(m_new, jnp.max(s, axis=-1, keepdims=True))
    alpha = jnp.exp(m_old - m_new)
    m_ref[...] = m_new
    l_new = alpha * l_ref[...]
    p_pages = []
    for s in s_pages:
        p = jnp.exp(s - m_new)
        l_new = l_new + jnp.sum(p, axis=-1, keepdims=True)
        p_pages.append(p)
    l_ref[...] = l_new
    for h in range(FOX_HEADS):
        acc = acc_ref[h] * alpha[h:h + 1, :]
        for pg in range(g):
            acc = acc + p_pages[pg][h:h + 1, :] * v_refs[pg][0, h]
        acc_ref[h] = acc

    @pl.when(step == pl.num_programs(1) - 1)
    def _():
        inv = 1.0 / l_ref[...]
        for h in range(FOX_HEADS):
            o_ref[0, h * FOX_HD:(h + 1) * FOX_HD, :] = jnp.sum(acc_ref[h], axis=-1, keepdims=True) * inv[h:h + 1, :]


def _decode(page_table, qc, kc, vc, fc, bf_col, ck, cv, clf):
    nd, n_pages = page_table.shape
    g = PAGES_PER_STEP
    steps = n_pages // g
    rows = ck.shape[3]

    def page_map(pg):
        return lambda b, s, pt: (pt[b, n_pages - 1 - (s * g + pg)], 0, 0, 0)

    def lf_map(pg):
        return lambda b, s, pt: (pt[b, n_pages - 1 - (s * g + pg)], 0, 0)

    per = lambda a: pl.BlockSpec((1,) + a.shape[1:], lambda b, s, pt: (b,) + (0,) * (a.ndim - 1))
    in_specs = [per(qc), per(kc), per(vc), per(fc), pl.BlockSpec(bf_col.shape, lambda b, s, pt: (0, 0))]
    in_specs += [pl.BlockSpec((1, FOX_HEADS, FOX_HD, rows), page_map(pg)) for pg in range(g)]
    in_specs += [pl.BlockSpec((1, FOX_HEADS, FOX_HD, rows), page_map(pg)) for pg in range(g)]
    in_specs += [pl.BlockSpec((1, FOX_HEADS, rows), lf_map(pg)) for pg in range(g)]
    out_shape = [jax.ShapeDtypeStruct((nd, MIX_HALF, 1), F32), jax.ShapeDtypeStruct((nd, FOX_HEADS, 1), F32)]
    out_specs = [pl.BlockSpec((1, MIX_HALF, 1), lambda b, s, pt: (b, 0, 0)),
                 pl.BlockSpec((1, FOX_HEADS, 1), lambda b, s, pt: (b, 0, 0))]
    grid_spec = pltpu.PrefetchScalarGridSpec(
        num_scalar_prefetch=1, grid=(nd, steps), in_specs=in_specs, out_specs=out_specs,
        scratch_shapes=[pltpu.VMEM((FOX_HEADS, 1), F32), pltpu.VMEM((FOX_HEADS, 1), F32),
                        pltpu.VMEM((FOX_HEADS, 1), F32), pltpu.VMEM((FOX_HEADS, FOX_HD, rows), F32)])
    return pl.pallas_call(
        _decode_kernel,
        out_shape=out_shape,
        grid_spec=grid_spec,
        compiler_params=_cparams("parallel", "arbitrary"),
        name="fox_decode",
    )(page_table, qc, kc, vc, fc, bf_col, *([ck] * g), *([cv] * g), *([clf] * g))


def _tile_rows(n, pref):
    t = min(n, pref)
    while n % t:
        t //= 2
    return t


def _cols(a):
    return jnp.transpose(a, (2, 1, 0))


def kernel(x_prompt, x_sample, state_conv_a, state_gla, cache_k, cache_v, cache_logf, state_conv_d, page_table,
           ab_w_in, ab_conv_w, gla_w_gate2, gla_b_gate, gla_norm_g, ab_w_out,
           cd_w_in, fox_b_f, cf_conv_w, cf_conv_b, cf_ln_g, cf_ln_b, cd_w_out,
           ffn_w_in, ffn_w_out, ln_g, ln_b):
    nb, seq, d = x_prompt.shape
    nd = x_sample.shape[0]
    hk = GLA_HEADS * GLA_DK

    o_bg, o_cg, o_hv = 0, MIX_HALF, 2 * MIX_HALF
    o_q = 3 * MIX_HALF
    o_k = o_q + hk
    o_v = o_k + hk
    o_g = o_v + MIX_HALF
    o_r = o_g + MIX_HALF
    wab = ab_w_in.astype(BF16)
    ab_wn = jnp.concatenate([wab[:, o_bg:o_q], wab[:, o_v:o_g], wab[:, o_g:o_r], wab[:, o_q:o_k],
                             jnp.pad(wab[:, o_r:o_r + GLA_RANK], ((0, 0), (0, R_PAD - GLA_RANK)))], axis=1)
    wab_t = wab.T
    ab_wt_p = jnp.concatenate([wab_t[o_k:o_v], wab_t[o_r:o_r + GLA_RANK]], axis=0)
    ab_wt_s = jnp.concatenate([ab_wt_p, wab_t[o_q:o_k]], axis=0)
    w2 = jnp.pad(gla_w_gate2.astype(BF16), ((0, R_PAD - GLA_RANK), (0, 0)))
    w2t = gla_w_gate2.T.astype(BF16)
    bg2 = gla_b_gate.reshape(1, hk)
    bg2c = gla_b_gate.reshape(hk, 1)
    ng = gla_norm_g.reshape(1, GLA_DV)
    ab_wo = ab_w_out.astype(BF16)

    wcd = cd_w_in.astype(BF16)
    c_q, c_k, c_v = 0, MIX_HALF, 2 * MIX_HALF
    c_f = 3 * MIX_HALF
    c_a = c_f + FOX_HEADS
    c_gate = c_a + MIX_HALF
    cd_wn = jnp.concatenate([wcd[:, c_q:c_k], wcd[:, c_a:c_gate], wcd[:, c_gate:c_gate + MIX_HALF]], axis=1)
    wcd_t = wcd.T
    cd_wt_p = jnp.concatenate([wcd_t[c_k:c_v], wcd_t[c_v:c_f],
                               jnp.pad(wcd_t[c_f:c_a], ((0, F_PAD - FOX_HEADS), (0, 0)))], axis=0)
    cd_wt_s = jnp.concatenate([cd_wt_p, wcd_t[c_q:c_k]], axis=0)
    bf_col = fox_b_f.reshape(FOX_HEADS, 1)
    cb = cf_conv_b.reshape(1, MIX_HALF)
    cg_ = cf_ln_g.reshape(1, MIX_HALF)
    cbb = cf_ln_b.reshape(1, MIX_HALF)
    cd_wo = cd_w_out.astype(BF16)
    ffn_wi = ffn_w_in.astype(BF16)
    ffn_wo = ffn_w_out.astype(BF16)
    tf = D_FF // 2

    def post(x2d, h1, c1, h2, c2, wo, layer, tm):
        x1 = _mm_res_ln(h1, c1, h2, c2, wo, x2d, ln_g[layer, 0:1], ln_b[layer, 0:1], tm)
        return _ffn(x1, ffn_wi[layer], ffn_wo[layer], ln_g[layer, 1:2], ln_b[layer, 1:2], tm, tf)

    m = nb * seq
    tm = _tile_rows(m, 512)
    tl = _tile_rows(seq, 512)
    x0 = x_prompt.reshape(m, d)
    pn, kt, rt = _proj(x0, ab_wn, ab_wt_p, (hk, GLA_RANK), nb, tl)
    mixed, conv_a_p, gla_p = _mix0(pn, kt, rt, ab_conv_w, w2, w2t, bg2, bg2c, ng, nb)
    x2 = post(x0, mixed, 0, mixed, 1, ab_wo, 0, tm)

    pn1, kt1, vt1, ft1 = _proj(x2, cd_wn, cd_wt_p, (MIX_HALF, MIX_HALF, F_PAD), nb, tl)
    lft, ct = _fox_c(ft1, bf_col)
    o_c = _fox_flash(pn1, kt1, vt1, ct, nb, tl)
    y_d, conv_d_p = _conf(pn1, cf_conv_w, cb, cg_, cbb, nb, _tile_rows(seq, 256))
    y_p = post(x2, o_c, 0, y_d, 0, cd_wo, 1, tm)

    y_prompt = y_p.reshape(nb, seq, d)
    k_p = jnp.transpose(kt1.reshape(nb, FOX_HEADS, FOX_HD, seq), (0, 3, 1, 2))
    v_p = jnp.transpose(vt1.reshape(nb, FOX_HEADS, FOX_HD, seq), (0, 3, 1, 2))
    lf_p = jnp.transpose(lft, (0, 2, 1))

    xs = x_sample.reshape(nd, d)
    pn_s, kt_s, rt_s, qt_s = _proj(xs, ab_wn, ab_wt_s, (hk, GLA_RANK, hk), 1, nd)
    mixed_s, conv_a_s, gla_s = _smix0(pn_s.reshape(nd, 1, -1), _cols(kt_s), _cols(qt_s), _cols(rt_s),
                                      state_conv_a, state_gla, ab_conv_w, w2t, bg2c, ng)
    mixed_s = mixed_s.reshape(nd, d)
    xs2 = post(xs, mixed_s, 0, mixed_s, 1, ab_wo, 0, nd)

    pn1_s, kt1_s, vt1_s, ft1_s, qt1_s = _proj(xs2, cd_wn, cd_wt_s, (MIX_HALF, MIX_HALF, F_PAD, MIX_HALF), 1, nd)
    ck = jnp.transpose(cache_k, (0, 2, 3, 1))
    cv = jnp.transpose(cache_v, (0, 2, 3, 1))
    clf = jnp.transpose(cache_logf, (0, 2, 1))
    o_col, lf_col = _decode(page_table, _cols(qt1_s), _cols(kt1_s), _cols(vt1_s), _cols(ft1_s[:, 0:FOX_HEADS]),
                            bf_col, ck, cv, clf)
    o_cs = o_col.reshape(nd, MIX_HALF).astype(BF16)
    y_ds, conv_d_s = _sconf(pn1_s.reshape(nd, 1, -1), state_conv_d, cf_conv_w, cb, cg_, cbb)
    y_s = post(xs2, o_cs, 0, y_ds.reshape(nd, MIX_HALF), 0, cd_wo, 1, nd)

    y_sample = y_s.reshape(nd, 1, d)
    k_s = jnp.transpose(kt1_s[0], (1, 0)).reshape(nd, 1, FOX_HEADS, FOX_HD)
    v_s = jnp.transpose(vt1_s[0], (1, 0)).reshape(nd, 1, FOX_HEADS, FOX_HD)
    lf_s = lf_col.reshape(nd, 1, FOX_HEADS)

    return (y_prompt, y_sample, conv_a_p, conv_a_s, gla_p, gla_s, k_p, k_s, v_p, v_s, lf_p, lf_s,
            conv_d_p, conv_d_s)
```

```python
import functools

import numpy as np
import jax
import jax.numpy as jnp
from jax import lax
from jax.experimental import pallas as pl
from jax.experimental.pallas import tpu as pltpu

F32 = jnp.float32
BF16 = jnp.bfloat16

D_MODEL = 1024
MIX_HALF = D_MODEL // 2
SC_K = 3
GLA_HEADS = 4
GLA_DV = MIX_HALF // GLA_HEADS
GLA_DK = GLA_DV // 2
GLA_RANK = 16
GLA_INV_TAU = 1.0 / 16.0
GLA_CHUNK = 128
GLA_LEVELS = 7
FOX_HEADS = 8
FOX_HD = MIX_HALF // FOX_HEADS
CF_K = 31
D_FF = 2816
DEPTH = 2
DN_ALPHA = (2 * DEPTH) ** 0.25
EPS = 1e-5
R_PAD = 128
F_PAD = 16
VMEM_LIMIT = 56 * 1024 * 1024
PAGES_PER_STEP = 16


def _cparams(*sem):
    return pltpu.CompilerParams(dimension_semantics=sem, vmem_limit_bytes=VMEM_LIMIT)


def _log_sigmoid(x):
    return jnp.minimum(x, 0.0) - jnp.log1p(jnp.exp(-jnp.abs(x)))


def _sigmoid(x):
    return 1.0 / (1.0 + jnp.exp(-x))


def _silu(x):
    return x * _sigmoid(x)


def _layernorm(y, g, b):
    mu = jnp.mean(y, axis=-1, keepdims=True)
    d = y - mu
    var = jnp.mean(d * d, axis=-1, keepdims=True)
    return d * lax.rsqrt(var + EPS) * g + b


def _dot(a, b):
    return jnp.dot(a, b, preferred_element_type=F32)


def _dot_nt(a, b):
    return lax.dot_general(a, b, (((1,), (1,)), ((), ())), preferred_element_type=F32)


def _proj_kernel(t_outs, x_ref, wn_ref, wt_ref, pn_ref, *pt_refs):
    xb = x_ref[...].astype(BF16)
    pn_ref[...] = _dot(xb, wn_ref[...])
    pt = _dot_nt(wt_ref[...], xb)
    for ref, (r0, n, dt) in zip(pt_refs, t_outs):
        ref[0] = pt[r0:r0 + n].astype(dt)


def _proj(x2d, wn, wt, t_outs, nb, tl):
    m, k = x2d.shape
    l = m // nb
    nl = l // tl
    nn = wn.shape[1]
    nt = wt.shape[0]
    out_shape = [jax.ShapeDtypeStruct((m, nn), F32)] + [jax.ShapeDtypeStruct((nb, n, l), dt) for _, n, dt in t_outs]
    out_specs = [pl.BlockSpec((tl, nn), lambda b, i: (b * nl + i, 0))] + [
        pl.BlockSpec((1, n, tl), lambda b, i: (b, 0, i)) for _, n, _ in t_outs]
    return pl.pallas_call(
        functools.partial(_proj_kernel, t_outs),
        out_shape=out_shape,
        grid=(nb, nl),
        in_specs=[pl.BlockSpec((tl, k), lambda b, i: (b * nl + i, 0)),
                  pl.BlockSpec((k, nn), lambda b, i: (0, 0)),
                  pl.BlockSpec((nt, k), lambda b, i: (0, 0))],
        out_specs=out_specs,
        compiler_params=_cparams("parallel", "arbitrary"),
        name="proj",
    )(x2d, wn, wt)


def _mm_res_ln_kernel(h1_ref, h2_ref, w_ref, x_ref, g_ref, b_ref, o_ref):
    half = h1_ref.shape[1]
    m = _dot(h1_ref[...], w_ref[0:half, :]) + _dot(h2_ref[...], w_ref[half:2 * half, :])
    o_ref[...] = _layernorm(DN_ALPHA * x_ref[...] + m, g_ref[...], b_ref[...])


def _mm_res_ln(h1, h1_col, h2, h2_col, w, x2d, g, b, tm):
    m, d = x2d.shape
    half = d // 2
    return pl.pallas_call(
        _mm_res_ln_kernel,
        out_shape=jax.ShapeDtypeStruct((m, d), F32),
        grid=(m // tm,),
        in_specs=[pl.BlockSpec((tm, half), lambda i: (i, h1_col)),
                  pl.BlockSpec((tm, half), lambda i: (i, h2_col)),
                  pl.BlockSpec((d, d), lambda i: (0, 0)),
                  pl.BlockSpec((tm, d), lambda i: (i, 0)),
                  pl.BlockSpec((1, d), lambda i: (0, 0)),
                  pl.BlockSpec((1, d), lambda i: (0, 0))],
        out_specs=pl.BlockSpec((tm, d), lambda i: (i, 0)),
        compiler_params=_cparams("parallel"),
        name="mm_res_ln",
    )(h1, h2, w, x2d, g, b)


def _ffn_kernel(x_ref, wg_ref, wu_ref, wo_ref, g_ref, b_ref, o_ref, acc_ref, xb_ref):
    j = pl.program_id(1)

    @pl.when(j == 0)
    def _():
        acc_ref[...] = jnp.zeros_like(acc_ref)
        xb_ref[...] = x_ref[...].astype(BF16)

    xb = xb_ref[...]
    gate = _dot(xb, wg_ref[...])
    up = _dot(xb, wu_ref[...])
    h = (_silu(gate) * up).astype(BF16)
    acc_ref[...] += _dot(h, wo_ref[...])

    @pl.when(j == pl.num_programs(1) - 1)
    def _():
        o_ref[...] = _layernorm(DN_ALPHA * x_ref[...] + acc_ref[...], g_ref[...], b_ref[...])


def _ffn(x2d, w_in, w_out, g, b, tm, tf):
    m, d = x2d.shape
    dff = w_out.shape[0]
    nf = dff // tf
    return pl.pallas_call(
        _ffn_kernel,
        out_shape=jax.ShapeDtypeStruct((m, d), F32),
        grid=(m // tm, nf),
        in_specs=[pl.BlockSpec((tm, d), lambda i, j: (i, 0)),
                  pl.BlockSpec((d, tf), lambda i, j: (0, j)),
                  pl.BlockSpec((d, tf), lambda i, j: (0, j + nf)),
                  pl.BlockSpec((tf, d), lambda i, j: (j, 0)),
                  pl.BlockSpec((1, d), lambda i, j: (0, 0)),
                  pl.BlockSpec((1, d), lambda i, j: (0, 0))],
        out_specs=pl.BlockSpec((tm, d), lambda i, j: (i, 0)),
        scratch_shapes=[pltpu.VMEM((tm, d), F32), pltpu.VMEM((tm, d), BF16)],
        compiler_params=_cparams("parallel", "arbitrary"),
        name="ffn",
    )(x2d, w_in, w_in, w_out, g, b)


def _gla_sum_matrices():
    c = GLA_CHUNK
    t = np.arange(c)[:, None]
    u = np.arange(c)[None, :]
    mq = [(u <= t)]
    mk = [(u > t)]
    for l in range(1, GLA_LEVELS + 1):
        blk, half = 1 << l, 1 << (l - 1)
        same = (t // blk) == (u // blk)
        mq.append(same & (t % blk >= half) & (u % blk >= half) & (u <= t))
        mk.append(same & (t % blk < half) & (u % blk < half) & (u > t))
    mk.append(np.ones((c, c), bool))
    mq = np.concatenate(mq, axis=0).astype(np.float32)
    mk = np.concatenate(mk, axis=0).astype(np.float32).T
    mq2 = np.concatenate([mq, mq], axis=1)
    mk2 = np.concatenate([mk, mk], axis=0)
    return jnp.asarray(mq2, BF16), jnp.asarray(mk2, BF16)


def _split_hi_lo(x):
    hi = x.astype(BF16)
    lo = (x - hi.astype(F32)).astype(BF16)
    return hi, lo


def _mix0_kernel(bg_ref, cg_ref, hv_ref, v_ref, gt_ref, q_ref, r_ref, kt_ref, rt_ref,
                 convw_ref, w2_ref, w2t_ref, bg2_ref, bg2c_ref, ng_ref, mq_ref, mk_ref,
                 mixed_ref, convst_ref, glast_ref, s_ref, ext_ref):
    c = GLA_CHUNK
    i = pl.program_id(1)

    @pl.when(i == 0)
    def _():
        s_ref[...] = jnp.zeros_like(s_ref)
        ext_ref[0:8, :] = jnp.zeros((8, MIX_HALF), F32)

    u = cg_ref[...] * hv_ref[...]
    ext_ref[8:8 + c, :] = u
    w = convw_ref[...]
    y = w[2:3, :] * u + w[1:2, :] * ext_ref[7:7 + c, :] + w[0:1, :] * ext_ref[6:6 + c, :]
    mixed_ref[:, 0:MIX_HALF] = (bg_ref[...] * y).astype(BF16)
    ext_ref[0:8, :] = u[c - 8:c, :]
    convst_ref[0] = u[c - (SC_K - 1):c, :]

    z = _dot(r_ref[...].astype(BF16), w2_ref[...]) + bg2_ref[...]
    la = _log_sigmoid(z) * GLA_INV_TAU
    zt = _dot(w2t_ref[...], rt_ref[0].astype(BF16)) + bg2c_ref[...]
    lat = _log_sigmoid(zt) * GLA_INV_TAU
    hi, lo = _split_hi_lo(la)
    eq = _dot(mq_ref[...], jnp.concatenate([hi, lo], axis=0))
    hit, lot = _split_hi_lo(lat)
    ek = _dot(jnp.concatenate([hit, lot], axis=1), mk_ref[...])

    q = q_ref[...] * (GLA_DK ** -0.5)
    kt = kt_ref[0]
    tt = lax.broadcasted_iota(jnp.int32, (c, c), 0)
    ss = lax.broadcasted_iota(jnp.int32, (c, c), 1)
    xr = tt ^ ss
    lower = tt > ss
    scores = []
    for h in range(GLA_HEADS):
        ks = slice(h * GLA_DK, (h + 1) * GLA_DK)
        qh = q[:, ks]
        kth = kt[ks, :]
        a = jnp.where(xr == 0, _dot(qh.astype(BF16), kth.astype(BF16)), 0.0)
        for l in range(1, GLA_LEVELS + 1):
            ql = (qh * jnp.exp(eq[l * c:(l + 1) * c, ks])).astype(BF16)
            kl = (kth * jnp.exp(ek[ks, l * c:(l + 1) * c])).astype(BF16)
            lvl = lower & (xr >= (1 << (l - 1))) & (xr < (1 << l))
            a = a + jnp.where(lvl, _dot(ql, kl), 0.0)
        scores.append(a.astype(BF16))
    for h in range(GLA_HEADS):
        ks = slice(h * GLA_DK, (h + 1) * GLA_DK)
        vs = slice(h * GLA_DV, (h + 1) * GLA_DV)
        qh = q[:, ks]
        kth = kt[ks, :]
        vh = v_ref[:, vs].astype(BF16)
        s_old = s_ref[h]
        q_read = (qh * jnp.exp(eq[0:c, ks])).astype(BF16)
        o = _dot(scores[h], vh) + _dot(q_read, s_old.astype(BF16))
        k_write = (kth * jnp.exp(ek[ks, 0:c])).astype(BF16)
        a_chunk = jnp.exp(ek[ks, (GLA_LEVELS + 1) * c:(GLA_LEVELS + 2) * c])
        s_new = a_chunk * s_old + _dot(k_write, vh)
        s_ref[h] = s_new
        glast_ref[0, h] = s_new
        on = o * lax.rsqrt(jnp.mean(o * o, axis=-1, keepdims=True) + EPS) * ng_ref[...]
        mixed_ref[:, MIX_HALF + h * GLA_DV:MIX_HALF + (h + 1) * GLA_DV] = (on * _silu(gt_ref[:, vs])).astype(BF16)


def _mix0(pn, kt, rt, conv_w, w2, w2t, bg2, bg2c, ng, nb):
    m = pn.shape[0]
    l = m // nb
    c = GLA_CHUNK
    nl = l // c
    mq, mk = _gla_sum_matrices()
    row = lambda b, i: b * nl + i
    full = lambda a: pl.BlockSpec(a.shape, lambda b, i: (0,) * a.ndim)
    wide = lambda col: pl.BlockSpec((c, MIX_HALF), lambda b, i: (row(b, i), col))
    q_col = 5 * MIX_HALF // (GLA_HEADS * GLA_DK)
    r_col = (5 * MIX_HALF + GLA_HEADS * GLA_DK) // R_PAD
    return pl.pallas_call(
        _mix0_kernel,
        out_shape=[jax.ShapeDtypeStruct((m, D_MODEL), BF16),
                   jax.ShapeDtypeStruct((nb, SC_K - 1, MIX_HALF), F32),
                   jax.ShapeDtypeStruct((nb, GLA_HEADS, GLA_DK, GLA_DV), F32)],
        grid=(nb, nl),
        in_specs=[wide(0), wide(1), wide(2), wide(3), wide(4),
                  pl.BlockSpec((c, GLA_HEADS * GLA_DK), lambda b, i: (row(b, i), q_col)),
                  pl.BlockSpec((c, R_PAD), lambda b, i: (row(b, i), r_col)),
                  pl.BlockSpec((1, GLA_HEADS * GLA_DK, c), lambda b, i: (b, 0, i)),
                  pl.BlockSpec((1, GLA_RANK, c), lambda b, i: (b, 0, i)),
                  full(conv_w), full(w2), full(w2t), full(bg2), full(bg2c), full(ng), full(mq), full(mk)],
        out_specs=[pl.BlockSpec((c, D_MODEL), lambda b, i: (row(b, i), 0)),
                   pl.BlockSpec((1, SC_K - 1, MIX_HALF), lambda b, i: (b, 0, 0)),
                   pl.BlockSpec((1, GLA_HEADS, GLA_DK, GLA_DV), lambda b, i: (b, 0, 0, 0))],
        scratch_shapes=[pltpu.VMEM((GLA_HEADS, GLA_DK, GLA_DV), F32), pltpu.VMEM((c + 8, MIX_HALF), F32)],
        compiler_params=_cparams("parallel", "arbitrary"),
        name="mix0",
    )(pn, pn, pn, pn, pn, pn, pn, kt, rt, conv_w, w2, w2t, bg2, bg2c, ng, mq, mk)


C_ROWS = 16
LOG2E = 1.4426950408889634


def _fox_c_kernel(ft_ref, bf_ref, lf_ref, c_ref):
    lf = _log_sigmoid(ft_ref[0] + bf_ref[...])
    lf_ref[0] = lf
    n = lf.shape[1]
    lane = lax.broadcasted_iota(jnp.int32, lf.shape, 1)
    suf = lf
    s = 1
    while s < n:
        suf = suf + jnp.where(lane + s < n, pltpu.roll(suf, n - s, 1), 0.0)
        s *= 2
    bias = (suf - lf) * LOG2E
    hi = bias.astype(BF16).astype(F32)
    r1 = bias - hi
    mid = r1.astype(BF16).astype(F32)
    lo = (r1 - mid).astype(BF16).astype(F32)
    row = lax.broadcasted_iota(jnp.int32, (C_ROWS, n), 0)
    for h in range(FOX_HEADS):
        hs = slice(h, h + 1)
        parts = jnp.where(row == 0, hi[hs], jnp.where(row == 1, mid[hs], jnp.where(row == 2, lo[hs], 0.0)))
        c_ref[0, h] = parts.astype(BF16)


def _fox_c(ft, bf_col):
    nb, _, l = ft.shape
    spec = pl.BlockSpec((1, FOX_HEADS, l), lambda b: (b, 0, 0))
    return pl.pallas_call(
        _fox_c_kernel,
        out_shape=[jax.ShapeDtypeStruct((nb, FOX_HEADS, l), F32),
                   jax.ShapeDtypeStruct((nb, FOX_HEADS, C_ROWS, l), BF16)],
        grid=(nb,),
        in_specs=[spec, pl.BlockSpec((FOX_HEADS, 1), lambda b: (0, 0))],
        out_specs=[spec, pl.BlockSpec((1, FOX_HEADS, C_ROWS, l), lambda b: (b, 0, 0, 0))],
        compiler_params=_cparams("parallel"),
        name="fox_c",
    )(ft, bf_col)


AUG = 2 * FOX_HD
FLASH_AHEAD = 8


def _fox_flash_kernel(q_ref, kt_ref, vt_ref, c_ref, o_ref, qa_ref, m_ref, acc_ref):
    qi = pl.program_id(1)
    kj = pl.program_id(2)
    tq = q_ref.shape[0]
    tk = kt_ref.shape[2]
    lane = lax.broadcasted_iota(jnp.int32, (tq, AUG), 1)

    @pl.when(kj == 0)
    def _():
        ones3 = jnp.where(lane < FOX_HD + 3, 1.0, 0.0)
        for p in range(FOX_HEADS // 2):
            qp = q_ref[:, p * AUG:(p + 1) * AUG] * (FOX_HD ** -0.5 * LOG2E)
            qa_ref[2 * p] = jnp.where(lane < FOX_HD, qp, ones3).astype(BF16)
            qa_ref[2 * p + 1] = jnp.where(lane < FOX_HD, pltpu.roll(qp, FOX_HD, 1), ones3).astype(BF16)
        m_ref[...] = jnp.full_like(m_ref, -jnp.inf)
        acc_ref[...] = jnp.zeros_like(acc_ref)

    def block(masked):
        if masked:
            visible = (lax.broadcasted_iota(jnp.int32, (tq, tk), 1) <= lax.broadcasted_iota(jnp.int32, (tq, tk), 0))
        pad_k = jnp.zeros((AUG - FOX_HD - C_ROWS, tk), BF16)
        ones_row = jnp.where(lax.broadcasted_iota(jnp.int32, (AUG - FOX_HD, tk), 0) == 0, 1.0, 0.0).astype(BF16)

        def logits(h):
            hs = slice(h * FOX_HD, (h + 1) * FOX_HD)
            ka = jnp.concatenate([kt_ref[0, hs, :], c_ref[0, h], pad_k], axis=0)
            return _dot(qa_ref[h], ka)

        s_ahead = [logits(h) for h in range(FLASH_AHEAD)]
        for h in range(FOX_HEADS):
            if h + FLASH_AHEAD < FOX_HEADS:
                s_ahead.append(logits(h + FLASH_AHEAD))
            s = s_ahead.pop(0)
            hs = slice(h * FOX_HD, (h + 1) * FOX_HD)
            va = jnp.concatenate([vt_ref[0, hs, :], ones_row], axis=0)
            if masked:
                s = jnp.where(visible, s, -jnp.inf)
            m_old = m_ref[h]
            m_new = jnp.maximum(m_old, jnp.max(s, axis=-1, keepdims=True))
            pr = jnp.exp2(s - m_new).astype(BF16)
            acc_ref[h] = jnp.exp2(m_old - m_new) * acc_ref[h] + _dot_nt(pr, va)
            m_ref[h] = m_new

    @pl.when(kj < qi)
    def _():
        block(False)

    @pl.when(kj == qi)
    def _():
        block(True)
        for p in range(FOX_HEADS // 2):
            a0 = acc_ref[2 * p]
            a1 = acc_ref[2 * p + 1]
            o0 = a0 * (1.0 / a0[:, FOX_HD:FOX_HD + 1])
            o1 = a1 * (1.0 / a1[:, FOX_HD:FOX_HD + 1])
            o_ref[:, p * AUG:(p + 1) * AUG] = jnp.where(lane < FOX_HD, o0, pltpu.roll(o1, FOX_HD, 1)).astype(BF16)


def _fox_flash(pn, kt, vt, caug, nb, tq):
    m = pn.shape[0]
    l = m // nb
    nq = l // tq
    kv_spec = pl.BlockSpec((1, MIX_HALF, tq), lambda b, i, j: (b, 0, jnp.minimum(i, j)))
    return pl.pallas_call(
        _fox_flash_kernel,
        out_shape=jax.ShapeDtypeStruct((m, MIX_HALF), BF16),
        grid=(nb, nq, nq),
        in_specs=[pl.BlockSpec((tq, MIX_HALF), lambda b, i, j: (b * nq + i, 0)),
                  kv_spec, kv_spec,
                  pl.BlockSpec((1, FOX_HEADS, C_ROWS, tq), lambda b, i, j: (b, 0, 0, jnp.minimum(i, j)))],
        out_specs=pl.BlockSpec((tq, MIX_HALF), lambda b, i, j: (b * nq + i, 0)),
        scratch_shapes=[pltpu.VMEM((FOX_HEADS, tq, AUG), BF16), pltpu.VMEM((FOX_HEADS, tq, 1), F32),
                        pltpu.VMEM((FOX_HEADS, tq, AUG), F32)],
        compiler_params=_cparams("parallel", "parallel", "arbitrary"),
        name="fox_flash",
    )(pn, kt, vt, caug)


CF_HALO = 32
SUBLANES = 8


def _conf_kernel(a_ref, gate_ref, w_ref, cb_ref, g_ref, b_ref, y_ref, st_ref, ext_ref):
    t = a_ref.shape[0]
    i = pl.program_id(1)

    @pl.when(i == 0)
    def _():
        ext_ref[0:CF_HALO, :] = jnp.zeros((CF_HALO, MIX_HALF), F32)

    u = a_ref[...] * _sigmoid(gate_ref[...])
    ext_ref[CF_HALO:CF_HALO + t, :] = u
    off = CF_HALO - (CF_K - 1)
    acc = jnp.zeros((t, MIX_HALF), F32) + cb_ref[...]
    for rho in range(SUBLANES):
        taps = [j for j in range(CF_K) if (off + j) % SUBLANES == rho]
        rows = t if rho == 0 else t + SUBLANES
        part = jnp.zeros((rows, MIX_HALF), F32)
        for j in taps:
            a0 = off + j - rho
            part = part + w_ref[j:j + 1, :] * ext_ref[a0:a0 + rows, :]
        acc = acc + part[rho:rho + t, :]
    y_ref[...] = _silu(_layernorm(acc, g_ref[...], b_ref[...])).astype(BF16)
    st_ref[0] = ext_ref[t + off:t + CF_HALO, :]
    ext_ref[0:CF_HALO, :] = ext_ref[t:t + CF_HALO, :]


def _conf(pn, w, cb, g, b, nb, t):
    m = pn.shape[0]
    l = m // nb
    nl = l // t
    full = lambda a: pl.BlockSpec(a.shape, lambda bb, i: (0,) * a.ndim)
    return pl.pallas_call(
        _conf_kernel,
        out_shape=[jax.ShapeDtypeStruct((m, MIX_HALF), BF16),
                   jax.ShapeDtypeStruct((nb, CF_K - 1, MIX_HALF), F32)],
        grid=(nb, nl),
        in_specs=[pl.BlockSpec((t, MIX_HALF), lambda bb, i: (bb * nl + i, 1)),
                  pl.BlockSpec((t, MIX_HALF), lambda bb, i: (bb * nl + i, 2)),
                  full(w), full(cb), full(g), full(b)],
        out_specs=[pl.BlockSpec((t, MIX_HALF), lambda bb, i: (bb * nl + i, 0)),
                   pl.BlockSpec((1, CF_K - 1, MIX_HALF), lambda bb, i: (bb, 0, 0))],
        scratch_shapes=[pltpu.VMEM((t + CF_HALO, MIX_HALF), F32)],
        compiler_params=_cparams("parallel", "arbitrary"),
        name="conformer",
    )(pn, pn, w, cb, g, b)


def _smix0_kernel(pn_ref, kc_ref, qc_ref, rc_ref, prev_ref, s_ref, convw_ref, w2t_ref, bg2c_ref, ng_ref,
                  mixed_ref, convst_ref, sout_ref):
    pn = pn_ref[0]
    bg, cg, hv = pn[:, 0:MIX_HALF], pn[:, MIX_HALF:2 * MIX_HALF], pn[:, 2 * MIX_HALF:3 * MIX_HALF]
    v, gt = pn[:, 3 * MIX_HALF:4 * MIX_HALF], pn[:, 4 * MIX_HALF:5 * MIX_HALF]
    u = cg * hv
    prev = prev_ref[0]
    w = convw_ref[...]
    y = w[0:1, :] * prev[0:1, :] + w[1:2, :] * prev[1:2, :] + w[2:3, :] * u
    mixed_ref[0, :, 0:MIX_HALF] = (bg * y).astype(BF16)
    convst_ref[0, 0:1, :] = prev[1:2, :]
    convst_ref[0, 1:2, :] = u

    rc = rc_ref[0].astype(BF16).astype(F32)
    w2t = w2t_ref[...].astype(F32)
    zc = bg2c_ref[...]
    for j in range(GLA_RANK):
        zc = zc + w2t[:, j:j + 1] * rc[j:j + 1, :]
    ac = jnp.exp(_log_sigmoid(zc) * GLA_INV_TAU)
    qc = qc_ref[0] * (GLA_DK ** -0.5)
    kc = kc_ref[0]
    for h in range(GLA_HEADS):
        ks = slice(h * GLA_DK, (h + 1) * GLA_DK)
        vs = slice(h * GLA_DV, (h + 1) * GLA_DV)
        s_new = ac[ks, :] * s_ref[0, h] + kc[ks, :] * v[:, vs]
        sout_ref[0, h] = s_new
        o = jnp.sum(qc[ks, :] * s_new, axis=0, keepdims=True)
        on = o * lax.rsqrt(jnp.mean(o * o, axis=-1, keepdims=True) + EPS) * ng_ref[...]
        mixed_ref[0, :, MIX_HALF + h * GLA_DV:MIX_HALF + (h + 1) * GLA_DV] = (on * _silu(gt[:, vs])).astype(BF16)


def _smix0(pn3, kc, qc, rc, prev, s0, conv_w, w2t, bg2c, ng):
    nd, _, nn = pn3.shape
    per = lambda a: pl.BlockSpec((1,) + a.shape[1:], lambda b: (b,) + (0,) * (a.ndim - 1))
    full = lambda a: pl.BlockSpec(a.shape, lambda b: (0,) * a.ndim)
    out_shape = [jax.ShapeDtypeStruct((nd, 1, D_MODEL), BF16),
                 jax.ShapeDtypeStruct((nd, SC_K - 1, MIX_HALF), F32),
                 jax.ShapeDtypeStruct(s0.shape, F32)]
    return pl.pallas_call(
        _smix0_kernel,
        out_shape=out_shape,
        grid=(nd,),
        in_specs=[per(pn3), per(kc), per(qc), per(rc), per(prev), per(s0),
                  full(conv_w), full(w2t), full(bg2c), full(ng)],
        out_specs=[per(o) for o in out_shape],
        compiler_params=_cparams("parallel"),
        name="smix0",
    )(pn3, kc, qc, rc, prev, s0, conv_w, w2t, bg2c, ng)


def _sconf_kernel(pn_ref, prev_ref, w_ref, cb_ref, g_ref, b_ref, y_ref, st_ref):
    pn = pn_ref[0]
    a, gate = pn[:, MIX_HALF:2 * MIX_HALF], pn[:, 2 * MIX_HALF:3 * MIX_HALF]
    u = a * _sigmoid(gate)
    prev = prev_ref[0]
    w = w_ref[...]
    acc = jnp.sum(w[0:CF_K - 1, :] * prev, axis=0, keepdims=True) + w[CF_K - 1:CF_K, :] * u + cb_ref[...]
    y_ref[0] = _silu(_layernorm(acc, g_ref[...], b_ref[...])).astype(BF16)
    st_ref[0, 0:CF_K - 2, :] = prev_ref[0, 1:CF_K - 1, :]
    st_ref[0, CF_K - 2:CF_K - 1, :] = u


def _sconf(pn3, prev, w, cb, g, b):
    nd = pn3.shape[0]
    per = lambda a: pl.BlockSpec((1,) + a.shape[1:], lambda bb: (bb,) + (0,) * (a.ndim - 1))
    full = lambda a: pl.BlockSpec(a.shape, lambda bb: (0,) * a.ndim)
    out_shape = [jax.ShapeDtypeStruct((nd, 1, MIX_HALF), BF16), jax.ShapeDtypeStruct(prev.shape, F32)]
    return pl.pallas_call(
        _sconf_kernel,
        out_shape=out_shape,
        grid=(nd,),
        in_specs=[per(pn3), per(prev), full(w), full(cb), full(g), full(b)],
        out_specs=[per(o) for o in out_shape],
        compiler_params=_cparams("parallel"),
        name="sconf",
    )(pn3, prev, w, cb, g, b)


def _decode_kernel(pt_ref, qc_ref, kc_ref, vc_ref, fc_ref, bf_ref, *refs):
    g = PAGES_PER_STEP
    k_refs, v_refs, lf_refs = refs[0:g], refs[g:2 * g], refs[2 * g:3 * g]
    o_ref, lfo_ref, m_ref, l_ref, r_ref, acc_ref = refs[3 * g:]
    step = pl.program_id(1)
    rows = k_refs[0].shape[3]
    scale = FOX_HD ** -0.5
    qc = qc_ref[0] * scale
    lane = lax.broadcasted_iota(jnp.int32, (FOX_HEADS, rows), 1)
    head_row = lax.broadcasted_iota(jnp.int32, (FOX_HEADS, rows), 0)

    @pl.when(step == 0)
    def _():
        lf_new = _log_sigmoid(fc_ref[0] + bf_ref[...])
        lfo_ref[0] = lf_new
        r_ref[...] = lf_new
        l_ref[...] = jnp.ones_like(l_ref)
        col0 = lax.broadcasted_iota(jnp.int32, (FOX_HD, rows), 1) == 0
        head_col = lax.broadcasted_iota(jnp.int32, (FOX_HEADS, 1), 0)
        s_self = jnp.zeros((FOX_HEADS, 1), F32)
        for h in range(FOX_HEADS):
            hs = slice(h * FOX_HD, (h + 1) * FOX_HD)
            s_h = jnp.sum(qc[hs, :] * kc_ref[0, hs, :], axis=0, keepdims=True)
            s_self = jnp.where(head_col == h, s_h, s_self)
            acc_ref[h] = jnp.where(col0, vc_ref[0, hs, :], 0.0)
        m_ref[...] = s_self

    r_run = r_ref[...]
    s_pages = []
    for pg in range(g):
        lf = lf_refs[pg][0]
        pre = lf
        sh = 1
        while sh < rows:
            pre = pre + jnp.where(lane >= sh, pltpu.roll(pre, sh, 1), 0.0)
            sh *= 2
        tot = pre[:, rows - 1:rows]
        bias = r_run + (tot - pre)
        r_run = r_run + tot
        s = jnp.zeros((FOX_HEADS, rows), F32)
        for h in range(FOX_HEADS):
            hs = slice(h * FOX_HD, (h + 1) * FOX_HD)
            sh_ = jnp.sum(qc[hs, :] * k_refs[pg][0, h], axis=0, keepdims=True)
            s = jnp.where(head_row == h, sh_, s)
        s_pages.append(s + bias)
    r_ref[...] = r_run

    m_old = m_ref[...]
    m_new = m_old
    for s in s_pages:
        m_new = jnp.maximum(m_new, jnp.max(s, axis=-1, keepdims=True))
    alpha = jnp.exp(m_old - m_new)
    m_ref[...] = m_new
    l_new = alpha * l_ref[...]
    p_pages = []
    for s in s_pages:
        p = jnp.exp(s - m_new)
        l_new = l_new + jnp.sum(p, axis=-1, keepdims=True)
        p_pages.append(p)
    l_ref[...] = l_new
    for h in range(FOX_HEADS):
        acc = acc_ref[h] * alpha[h:h + 1, :]
        for pg in range(g):
            acc = acc + p_pages[pg][h:h + 1, :] * v_refs[pg][0, h]
        acc_ref[h] = acc

    @pl.when(step == pl.num_programs(1) - 1)
    def _():
        inv = 1.0 / l_ref[...]
        for h in range(FOX_HEADS):
            o_ref[0, h * FOX_HD:(h + 1) * FOX_HD, :] = jnp.sum(acc_ref[h], axis=-1, keepdims=True) * inv[h:h + 1, :]


def _decode(page_table, qc, kc, vc, fc, bf_col, ck, cv, clf):
    nd, n_pages = page_table.shape
    g = PAGES_PER_STEP
    steps = n_pages // g
    rows = ck.shape[3]

    def page_map(pg):
        return lambda b, s, pt: (pt[b, n_pages - 1 - (s * g + pg)], 0, 0, 0)

    def lf_map(pg):
        return lambda b, s, pt: (pt[b, n_pages - 1 - (s * g + pg)], 0, 0)

    per = lambda a: pl.BlockSpec((1,) + a.shape[1:], lambda b, s, pt: (b,) + (0,) * (a.ndim - 1))
    in_specs = [per(qc), per(kc), per(vc), per(fc), pl.BlockSpec(bf_col.shape, lambda b, s, pt: (0, 0))]
    in_specs += [pl.BlockSpec((1, FOX_HEADS, FOX_HD, rows), page_map(pg)) for pg in range(g)]
    in_specs += [pl.BlockSpec((1, FOX_HEADS, FOX_HD, rows), page_map(pg)) for pg in range(g)]
    in_specs += [pl.BlockSpec((1, FOX_HEADS, rows), lf_map(pg)) for pg in range(g)]
    out_shape = [jax.ShapeDtypeStruct((nd, MIX_HALF, 1), F32), jax.ShapeDtypeStruct((nd, FOX_HEADS, 1), F32)]
    out_specs = [pl.BlockSpec((1, MIX_HALF, 1), lambda b, s, pt: (b, 0, 0)),
                 pl.BlockSpec((1, FOX_HEADS, 1), lambda b, s, pt: (b, 0, 0))]
    grid_spec = pltpu.PrefetchScalarGridSpec(
        num_scalar_prefetch=1, grid=(nd, steps), in_specs=in_specs, out_specs=out_specs,
        scratch_shapes=[pltpu.VMEM((FOX_HEADS, 1), F32), pltpu.VMEM((FOX_HEADS, 1), F32),
                        pltpu.VMEM((FOX_HEADS, 1), F32), pltpu.VMEM((FOX_HEADS, FOX_HD, rows), F32)])
    return pl.pallas_call(
        _decode_kernel,
        out_shape=out_shape,
        grid_spec=grid_spec,
        compiler_params=_cparams("parallel", "arbitrary"),
        name="fox_decode",
    )(page_table, qc, kc, vc, fc, bf_col, *([ck] * g), *([cv] * g), *([clf] * g))


def _tile_rows(n, pref):
    t = min(n, pref)
    while n % t:
        t //= 2
    return t


def _cols(a):
    return jnp.transpose(a, (2, 1, 0))


def kernel(x_prompt, x_sample, state_conv_a, state_gla, cache_k, cache_v, cache_logf, state_conv_d, page_table,
           ab_w_in, ab_conv_w, gla_w_gate2, gla_b_gate, gla_norm_g, ab_w_out,
           cd_w_in, fox_b_f, cf_conv_w, cf_conv_b, cf_ln_g, cf_ln_b, cd_w_out,
           ffn_w_in, ffn_w_out, ln_g, ln_b):
    nb, seq, d = x_prompt.shape
    nd = x_sample.shape[0]
    hk = GLA_HEADS * GLA_DK

    o_bg, o_cg, o_hv = 0, MIX_HALF, 2 * MIX_HALF
    o_q = 3 * MIX_HALF
    o_k = o_q + hk
    o_v = o_k + hk
    o_g = o_v + MIX_HALF
    o_r = o_g + MIX_HALF
    wab = ab_w_in.astype(BF16)
    ab_wn = jnp.concatenate([wab[:, o_bg:o_q], wab[:, o_v:o_g], wab[:, o_g:o_r], wab[:, o_q:o_k],
                             jnp.pad(wab[:, o_r:o_r + GLA_RANK], ((0, 0), (0, R_PAD - GLA_RANK)))], axis=1)
    wab_t = wab.T
    ab_wt_p = jnp.concatenate([wab_t[o_k:o_v], wab_t[o_r:o_r + GLA_RANK]], axis=0)
    ab_wt_s = jnp.concatenate([ab_wt_p, wab_t[o_q:o_k]], axis=0)
    w2 = jnp.pad(gla_w_gate2.astype(BF16), ((0, R_PAD - GLA_RANK), (0, 0)))
    w2t = gla_w_gate2.T.astype(BF16)
    bg2 = gla_b_gate.reshape(1, hk)
    bg2c = gla_b_gate.reshape(hk, 1)
    ng = gla_norm_g.reshape(1, GLA_DV)
    ab_wo = ab_w_out.astype(BF16)

    wcd = cd_w_in.astype(BF16)
    c_q, c_k, c_v = 0, MIX_HALF, 2 * MIX_HALF
    c_f = 3 * MIX_HALF
    c_a = c_f + FOX_HEADS
    c_gate = c_a + MIX_HALF
    cd_wn = jnp.concatenate([wcd[:, c_q:c_k], wcd[:, c_a:c_gate], wcd[:, c_gate:c_gate + MIX_HALF]], axis=1)
    wcd_t = wcd.T
    cd_wt_p = jnp.concatenate([wcd_t[c_k:c_v], wcd_t[c_v:c_f],
                               jnp.pad(wcd_t[c_f:c_a], ((0, F_PAD - FOX_HEADS), (0, 0)))], axis=0)
    cd_wt_s = jnp.concatenate([cd_wt_p, wcd_t[c_q:c_k]], axis=0)
    bf_col = fox_b_f.reshape(FOX_HEADS, 1)
    cb = cf_conv_b.reshape(1, MIX_HALF)
    cg_ = cf_ln_g.reshape(1, MIX_HALF)
    cbb = cf_ln_b.reshape(1, MIX_HALF)
    cd_wo = cd_w_out.astype(BF16)
    ffn_wi = ffn_w_in.astype(BF16)
    ffn_wo = ffn_w_out.astype(BF16)
    tf = D_FF // 2

    def post(x2d, h1, c1, h2, c2, wo, layer, tm):
        x1 = _mm_res_ln(h1, c1, h2, c2, wo, x2d, ln_g[layer, 0:1], ln_b[layer, 0:1], tm)
        return _ffn(x1, ffn_wi[layer], ffn_wo[layer], ln_g[layer, 1:2], ln_b[layer, 1:2], tm, tf)

    m = nb * seq
    tm = _tile_rows(m, 512)
    tl = _tile_rows(seq, 512)
    x0 = x_prompt.reshape(m, d)
    pn, kt, rt = _proj(x0, ab_wn, ab_wt_p, ((0, hk, F32), (hk, GLA_RANK, F32)), nb, tl)
    mixed, conv_a_p, gla_p = _mix0(pn, kt, rt, ab_conv_w, w2, w2t, bg2, bg2c, ng, nb)
    x2 = post(x0, mixed, 0, mixed, 1, ab_wo, 0, tm)

    kvf = ((0, MIX_HALF, F32), (MIX_HALF, MIX_HALF, F32), (2 * MIX_HALF, F_PAD, F32))
    pn1, kt1, vt1, ft1, kt16, vt16 = _proj(x2, cd_wn, cd_wt_p,
                                           kvf + ((0, MIX_HALF, BF16), (MIX_HALF, MIX_HALF, BF16)), nb, tl)
    lft, caug = _fox_c(ft1, bf_col)
    o_c = _fox_flash(pn1, kt16, vt16, caug, nb, tl)
    y_d, conv_d_p = _conf(pn1, cf_conv_w, cb, cg_, cbb, nb, _tile_rows(seq, 256))
    y_p = post(x2, o_c, 0, y_d, 0, cd_wo, 1, tm)

    y_prompt = y_p.reshape(nb, seq, d)
    k_p = jnp.transpose(kt1.reshape(nb, FOX_HEADS, FOX_HD, seq), (0, 3, 1, 2))
    v_p = jnp.transpose(vt1.reshape(nb, FOX_HEADS, FOX_HD, seq), (0, 3, 1, 2))
    lf_p = jnp.transpose(lft, (0, 2, 1))

    xs = x_sample.reshape(nd, d)
    pn_s, kt_s, rt_s, qt_s = _proj(xs, ab_wn, ab_wt_s,
                                   ((0, hk, F32), (hk, GLA_RANK, F32), (hk + GLA_RANK, hk, F32)), 1, nd)
    mixed_s, conv_a_s, gla_s = _smix0(pn_s.reshape(nd, 1, -1), _cols(kt_s), _cols(qt_s), _cols(rt_s),
                                      state_conv_a, state_gla, ab_conv_w, w2t, bg2c, ng)
    mixed_s = mixed_s.reshape(nd, d)
    xs2 = post(xs, mixed_s, 0, mixed_s, 1, ab_wo, 0, nd)

    pn1_s, kt1_s, vt1_s, ft1_s, qt1_s = _proj(xs2, cd_wn, cd_wt_s,
                                              kvf + ((2 * MIX_HALF + F_PAD, MIX_HALF, F32),), 1, nd)
    ck = jnp.transpose(cache_k, (0, 2, 3, 1))
    cv = jnp.transpose(cache_v, (0, 2, 3, 1))
    clf = jnp.transpose(cache_logf, (0, 2, 1))
    o_col, lf_col = _decode(page_table, _cols(qt1_s), _cols(kt1_s), _cols(vt1_s), _cols(ft1_s[:, 0:FOX_HEADS]),
                            bf_col, ck, cv, clf)
    o_cs = o_col.reshape(nd, MIX_HALF).astype(BF16)
    y_ds, conv_d_s = _sconf(pn1_s.reshape(nd, 1, -1), state_conv_d, cf_conv_w, cb, cg_, cbb)
    y_s = post(xs2, o_cs, 0, y_ds.reshape(nd, MIX_HALF), 0, cd_wo, 1, nd)

    y_sample = y_s.reshape(nd, 1, d)
    k_s = jnp.transpose(kt1_s[0], (1, 0)).reshape(nd, 1, FOX_HEADS, FOX_HD)
    v_s = jnp.transpose(vt1_s[0], (1, 0)).reshape(nd, 1, FOX_HEADS, FOX_HD)
    lf_s = lf_col.reshape(nd, 1, FOX_HEADS)

    return (y_prompt, y_sample, conv_a_p, conv_a_s, gla_p, gla_s, k_p, k_s, v_p, v_s, lf_p, lf_s,
            conv_d_p, conv_d_s)
```

```python
import functools

import numpy as np
import jax
import jax.numpy as jnp
from jax import lax
from jax.experimental import pallas as pl
from jax.experimental.pallas import tpu as pltpu

F32 = jnp.float32
BF16 = jnp.bfloat16

D_MODEL = 1024
MIX_HALF = D_MODEL // 2
SC_K = 3
GLA_HEADS = 4
GLA_DV = MIX_HALF // GLA_HEADS
GLA_DK = GLA_DV // 2
GLA_RANK = 16
GLA_INV_TAU = 1.0 / 16.0
GLA_CHUNK = 128
GLA_LEVELS = 7
MIX_ROWS = 512
FOX_HEADS = 8
FOX_HD = MIX_HALF // FOX_HEADS
CF_K = 31
D_FF = 2816
DEPTH = 2
DN_ALPHA = (2 * DEPTH) ** 0.25
EPS = 1e-5
R_PAD = 128
F_PAD = 16
VMEM_LIMIT = 56 * 1024 * 1024
PAGES_PER_STEP = 16
FLASH_PAGES = 16
TAIL_ROWS = 256
SEQ_TILE = 512
SAMPLE_SEQS_PER_STEP = 8


def _cparams(*sem):
    return pltpu.CompilerParams(dimension_semantics=sem, vmem_limit_bytes=VMEM_LIMIT)


def _log_sigmoid(x):
    return jnp.minimum(x, 0.0) - jnp.log1p(jnp.exp(-jnp.abs(x)))


def _sigmoid(x):
    return 1.0 / (1.0 + jnp.exp(-x))


def _silu(x):
    return x * _sigmoid(x)


def _layernorm(y, g, b):
    mu = jnp.mean(y, axis=-1, keepdims=True)
    d = y - mu
    var = jnp.mean(d * d, axis=-1, keepdims=True)
    return d * lax.rsqrt(var + EPS) * g + b


def _dot(a, b):
    return jnp.dot(a, b, preferred_element_type=F32)


def _dot_nt(a, b):
    return lax.dot_general(a, b, (((1,), (1,)), ((), ())), preferred_element_type=F32)


def _proj_kernel(t_outs, as_columns, x_ref, wn_ref, wt_ref, pn_ref, *pt_refs):
    xb = x_ref[...].astype(BF16)
    pn_ref[...] = _dot(xb, wn_ref[...])
    pt = _dot_nt(wt_ref[...], xb)
    for ref, (r0, n, dt) in zip(pt_refs, t_outs):
        if as_columns:
            for row in range(x_ref.shape[0]):
                ref[row] = pt[r0:r0 + n, row:row + 1].astype(dt)
        else:
            ref[0] = pt[r0:r0 + n].astype(dt)


def _proj(x2d, wn, wt, t_outs, nb, tl, as_columns=False):
    m, k = x2d.shape
    l = m // nb
    nl = l // tl
    nn = wn.shape[1]
    nt = wt.shape[0]
    out_shape = [jax.ShapeDtypeStruct((m, nn), F32)]
    out_specs = [pl.BlockSpec((tl, nn), lambda b, i: (b * nl + i, 0))]
    if as_columns:
        assert nb == 1 and nl == 1
        out_shape += [jax.ShapeDtypeStruct((m, n, 1), dt) for _, n, dt in t_outs]
        out_specs += [pl.BlockSpec((m, n, 1), lambda b, i: (0, 0, 0)) for _, n, _ in t_outs]
    else:
        out_shape += [jax.ShapeDtypeStruct((nb, n, l), dt) for _, n, dt in t_outs]
        out_specs += [pl.BlockSpec((1, n, tl), lambda b, i: (b, 0, i)) for _, n, _ in t_outs]
    return pl.pallas_call(
        functools.partial(_proj_kernel, t_outs, as_columns),
        out_shape=out_shape,
        grid=(nb, nl),
        in_specs=[pl.BlockSpec((tl, k), lambda b, i: (b * nl + i, 0)),
                  pl.BlockSpec((k, nn), lambda b, i: (0, 0)),
                  pl.BlockSpec((nt, k), lambda b, i: (0, 0))],
        out_specs=out_specs,
        compiler_params=_cparams("parallel", "arbitrary"),
        name="proj",
    )(x2d, wn, wt)


LANES = 128


def _lane_sum_mxu_wide(x):
    ones = jnp.ones((x.shape[1], LANES), BF16)
    hi, lo = _split_hi_lo(x)
    return _dot(hi, ones) + _dot(lo, ones)


def _lane_sum_mxu(x):
    return _lane_sum_mxu_wide(x)[:, 0:1]


def _per_head_rows(a):
    return jnp.concatenate([jnp.broadcast_to(a[h:h + 1, :], (FOX_HD, 1)) for h in range(FOX_HEADS)], axis=0)


def _decode_init(step, qc_ref, kc_ref, vc_ref, fc_ref, bf_ref, k_refs, v_refs, lf_refs,
                 o_ref, lfo_ref, m_ref, l_ref, r_ref, acc_ref):
    @pl.when(step == 0)
    def _():
        qc = qc_ref[0] * (FOX_HD ** -0.5)
        lf_new = _log_sigmoid(fc_ref[0] + bf_ref[...])
        lfo_ref[0] = lf_new
        r_ref[...] = lf_new
        l_ref[...] = jnp.ones_like(l_ref)
        head_col = lax.broadcasted_iota(jnp.int32, (FOX_HEADS, 1), 0)
        s_self = jnp.zeros((FOX_HEADS, 1), F32)
        for h in range(FOX_HEADS):
            hs = slice(h * FOX_HD, (h + 1) * FOX_HD)
            s_h = jnp.sum(qc[hs, :] * kc_ref[0, hs, :], axis=0, keepdims=True)
            s_self = jnp.where(head_col == h, s_h, s_self)
        m_ref[...] = s_self
        acc_ref[...] = vc_ref[0]


def _decode_logits(qc_ref, kc_ref, vc_ref, fc_ref, bf_ref, k_refs, v_refs, lf_refs,
                   o_ref, lfo_ref, m_ref, l_ref, r_ref, acc_ref):
    rows = k_refs[0].shape[-1]
    qc = qc_ref[0] * (FOX_HD ** -0.5)
    lane = lax.broadcasted_iota(jnp.int32, (FOX_HEADS, rows), 1)
    head_row = lax.broadcasted_iota(jnp.int32, (FOX_HEADS, rows), 0)
    q_wide = [jnp.broadcast_to(qc[h * FOX_HD:(h + 1) * FOX_HD, :], (FOX_HD, rows)) for h in range(FOX_HEADS)]
    r_run = r_ref[...]
    s_pages = []
    for k_ref, lf_ref in zip(k_refs, lf_refs):
        lf = lf_ref[...]
        pre = lf
        sh = 1
        while sh < rows:
            pre = pre + jnp.where(lane >= sh, pltpu.roll(pre, sh, 1), 0.0)
            sh *= 2
        tot = pre[:, rows - 1:rows]
        bias = r_run + (tot - pre)
        r_run = r_run + tot
        s = jnp.zeros((FOX_HEADS, rows), F32)
        for h in range(FOX_HEADS):
            s_h = jnp.sum(q_wide[h] * k_ref[h], axis=0, keepdims=True)
            s = jnp.where(head_row == h, s_h, s)
        s_pages.append(s + bias)
    r_ref[...] = r_run

    m_old = m_ref[...]
    m_new = m_old
    for s in s_pages:
        m_new = jnp.maximum(m_new, jnp.max(s, axis=-1, keepdims=True))
    alpha = jnp.exp(m_old - m_new)
    m_ref[...] = m_new
    p_pages = [jnp.exp(s - m_new) for s in s_pages]
    p_sum = p_pages[0]
    for p in p_pages[1:]:
        p_sum = p_sum + p
    l_ref[...] = alpha * l_ref[...] + _lane_sum_mxu(p_sum)
    return alpha, p_pages


def _decode_values(alpha, p_pages, v_refs, acc_ref):
    parts = []
    for h in range(FOX_HEADS):
        acc = p_pages[0][h:h + 1, :] * v_refs[0][h]
        for p, v_ref in zip(p_pages[1:], v_refs[1:]):
            acc = acc + p[h:h + 1, :] * v_ref[h]
        parts.append(acc)
    acc_ref[...] = _per_head_rows(alpha) * acc_ref[...] + _lane_sum_mxu(jnp.concatenate(parts, axis=0))


def _decode_finish(step, n_steps, o_ref, l_ref, acc_ref):
    @pl.when(step == n_steps - 1)
    def _():
        o_ref[0] = acc_ref[...] * _per_head_rows(1.0 / l_ref[...])


N_DEC_IN = 8
N_DEC_SCRATCH = 8


def _decode_specs(arrs, seq_of, rows, g):
    per = lambda a: pl.BlockSpec((1,) + a.shape[1:], lambda *i: (seq_of(*i[:-1]),) + (0,) * (a.ndim - 1))
    qc, kc, vc, fc, bf_col = arrs
    in_specs = [per(qc), per(kc), per(vc), per(fc), pl.BlockSpec(bf_col.shape, lambda *i: (0, 0))]
    in_specs += [pl.BlockSpec(memory_space=pl.ANY)] * 3
    scratch = [pltpu.VMEM((FOX_HEADS, 1), F32), pltpu.VMEM((FOX_HEADS, 1), F32),
               pltpu.VMEM((FOX_HEADS, 1), F32), pltpu.VMEM((MIX_HALF, 1), F32),
               pltpu.VMEM((2, g, FOX_HEADS, FOX_HD, rows), F32), pltpu.VMEM((2, g, FOX_HEADS, FOX_HD, rows), F32),
               pltpu.VMEM((2, g, FOX_HEADS, rows), F32), pltpu.SemaphoreType.DMA((2, 3))]
    return in_specs, scratch


def _decode_begin(pt_ref, ins, outs, scratch, u, n_steps, seq_of, sps, g):
    m_ref, l_ref, r_ref, acc_ref, kbuf, vbuf, lfbuf, sems = scratch
    ck, cv, clf = ins[5:]
    n_pages = pt_ref.shape[1]

    def copies(step, slot, pages_known=True):
        out = []
        for pg in range(g):
            page = pt_ref[seq_of(step), n_pages - 1 - ((step % sps) * g + pg)] if pages_known else 0
            out.append(pltpu.make_async_copy(ck.at[page], kbuf.at[slot, pg], sems.at[slot, 0]))
            out.append(pltpu.make_async_copy(cv.at[page], vbuf.at[slot, pg], sems.at[slot, 1]))
            out.append(pltpu.make_async_copy(clf.at[page], lfbuf.at[slot, pg], sems.at[slot, 2]))
        return out

    slot = u % 2

    @pl.when(u == 0)
    def _():
        for c in copies(u, slot):
            c.start()

    for c in copies(jnp.minimum(u + 1, n_steps - 1), 1 - slot):
        c.start()
    for c in copies(u, slot, pages_known=False):
        c.wait()
    views = tuple([buf.at[slot, pg] for pg in range(g)] for buf in (kbuf, vbuf, lfbuf))
    dec = (*ins[:5], *views, *outs, m_ref, l_ref, r_ref, acc_ref)

    def drain():
        @pl.when(u == n_steps - 1)
        def _():
            for c in copies(u, 1 - slot, pages_known=False):
                c.wait()

    return dec, drain


def _decode_outs(n_seq, seq_of):
    shapes = [jax.ShapeDtypeStruct((n_seq, MIX_HALF, 1), F32), jax.ShapeDtypeStruct((n_seq, FOX_HEADS, 1), F32)]
    specs = [pl.BlockSpec((1, MIX_HALF, 1), lambda *i: (seq_of(*i[:-1]), 0, 0)),
             pl.BlockSpec((1, FOX_HEADS, 1), lambda *i: (seq_of(*i[:-1]), 0, 0))]
    return shapes, specs


def _tail_kernel(seq0, steps_per_seq, *refs):
    n_dec = steps_per_seq
    if n_dec:
        pt_ref, refs = refs[0], refs[1:]
    h1_ref, h2_ref, wo_ref, x_ref, g1_ref, b1_ref, wg_ref, wu_ref, wd_ref, g2_ref, b2_ref = refs[:11]
    refs = refs[11:]
    if n_dec:
        dec_in, refs = refs[:N_DEC_IN], refs[N_DEC_IN:]
    o_ref = refs[0]
    if n_dec:
        u = pl.program_id(0)
        step = u % steps_per_seq
        dec, drain = _decode_begin(pt_ref, dec_in, refs[1:3], refs[3:], u, pl.num_programs(0),
                                   lambda s: seq0 + s // steps_per_seq, steps_per_seq, PAGES_PER_STEP)
        _decode_init(step, *dec)
    half = h1_ref.shape[1]
    mix = _dot(h1_ref[...], wo_ref[0:half, :]) + _dot(h2_ref[...], wo_ref[half:2 * half, :])
    if n_dec:
        alpha, p_pages = _decode_logits(*dec)
    x1 = _layernorm(DN_ALPHA * x_ref[...] + mix, g1_ref[...], b1_ref[...])
    xb = x1.astype(BF16)
    gate = _dot(xb, wg_ref[...])
    up = _dot(xb, wu_ref[...])
    if n_dec:
        _decode_values(alpha, p_pages, dec[6], dec[13])
    h = (_silu(gate) * up).astype(BF16)
    o_ref[...] = _layernorm(DN_ALPHA * x1 + _dot(h, wd_ref[...]), g2_ref[...], b2_ref[...])
    if n_dec:
        _decode_finish(step, steps_per_seq, dec[8], dec[11], dec[13])
        drain()


def _tail(h1, h1_col, h2, h2_col, wo, x2d, g1, b1, w_in, wd, layer, g2, b2, tm, dec=None):
    m, d = x2d.shape
    half = d // 2
    dff = wd.shape[1]
    nsteps = m // tm
    once = lambda shape, imap: pl.BlockSpec(shape, imap, pipeline_mode=pl.Buffered(1))
    in_specs = [pl.BlockSpec((tm, half), lambda i, *_: (i, h1_col)),
                pl.BlockSpec((tm, half), lambda i, *_: (i, h2_col)),
                once((d, d), lambda i, *_: (0, 0)),
                pl.BlockSpec((tm, d), lambda i, *_: (i, 0)),
                pl.BlockSpec((1, d), lambda i, *_: (0, 0)),
                pl.BlockSpec((1, d), lambda i, *_: (0, 0)),
                once((None, d, dff), lambda i, *_: (layer, 0, 0)),
                once((None, d, dff), lambda i, *_: (layer, 0, 1)),
                once((None, dff, d), lambda i, *_: (layer, 0, 0)),
                pl.BlockSpec((1, d), lambda i, *_: (0, 0)),
                pl.BlockSpec((1, d), lambda i, *_: (0, 0))]
    out_shape = [jax.ShapeDtypeStruct((m, d), F32)]
    out_specs = [pl.BlockSpec((tm, d), lambda i, *_: (i, 0))]
    args = [h1, h2, wo, x2d, g1, b1, w_in, w_in, wd, g2, b2]
    if dec is None:
        return pl.pallas_call(
            functools.partial(_tail_kernel, 0, 0), out_shape=out_shape, grid=(nsteps,), in_specs=in_specs,
            out_specs=out_specs, compiler_params=_cparams("parallel"), name="tail")(*args)[0]
    page_table, small, (ck, cv, clf), seq0, n_seq = dec
    n_pages = page_table.shape[1]
    g = PAGES_PER_STEP
    sps = n_pages // g
    assert nsteps == n_seq * sps
    seq_of = lambda i: seq0 + i // sps
    dec_specs, scratch = _decode_specs(small, seq_of, ck.shape[3], g)
    d_shapes, d_specs = _decode_outs(n_seq, lambda i: i // sps)
    grid_spec = pltpu.PrefetchScalarGridSpec(
        num_scalar_prefetch=1, grid=(nsteps,), in_specs=in_specs + dec_specs, out_specs=out_specs + d_specs,
        scratch_shapes=scratch)
    return pl.pallas_call(
        functools.partial(_tail_kernel, seq0, sps), out_shape=out_shape + d_shapes, grid_spec=grid_spec,
        compiler_params=_cparams("arbitrary"), name="tail_decode",
    )(page_table, *args, *small, ck, cv, clf)


def _gla_level_masks():
    c = GLA_CHUNK
    t = np.arange(c)[:, None]
    s = np.arange(c)[None, :]
    x = t ^ s
    masks = [x == 0] + [(t > s) & (x >= (1 << (l - 1))) & (x < (1 << l)) for l in range(1, GLA_LEVELS + 1)]
    m = np.stack(masks).astype(np.float32)
    return jnp.asarray(np.concatenate([m, m], axis=2))


def _gla_sum_matrices():
    c = GLA_CHUNK
    t = np.arange(c)[:, None]
    u = np.arange(c)[None, :]
    mq = [(u <= t)]
    mk = [(u > t)]
    for l in range(1, GLA_LEVELS + 1):
        blk, half = 1 << l, 1 << (l - 1)
        same = (t // blk) == (u // blk)
        mq.append(same & (t % blk >= half) & (u % blk >= half) & (u <= t))
        mk.append(same & (t % blk < half) & (u % blk < half) & (u > t))
    mk.append(np.ones((c, c), bool))
    mq = np.concatenate(mq, axis=0).astype(np.float32)
    mk = np.concatenate(mk, axis=0).astype(np.float32).T
    mq2 = np.concatenate([mq, mq], axis=1)
    mk2 = np.concatenate([mk, mk], axis=0)
    return jnp.asarray(mq2, BF16), jnp.asarray(mk2, BF16)


def _split_hi_lo(x):
    hi = x.astype(BF16)
    lo = (x - hi.astype(F32)).astype(BF16)
    return hi, lo


def _mix0_kernel(bg_ref, cg_ref, hv_ref, v_ref, gt_ref, q_ref, r_ref, kt_ref, rt_ref,
                 convw_ref, w2_ref, w2t_ref, bg2_ref, bg2c_ref, ng_ref, mq_ref, mk_ref, lvl_ref,
                 mixed_ref, convst_ref, glast_ref, s_ref, ext_ref):
    c = GLA_CHUNK
    rows = bg_ref.shape[0]
    i = pl.program_id(1)

    @pl.when(i == 0)
    def _():
        s_ref[...] = jnp.zeros_like(s_ref)
        ext_ref[0:8, :] = jnp.zeros((8, MIX_HALF), F32)

    u = cg_ref[...] * hv_ref[...]
    ext_ref[8:8 + rows, :] = u
    w = convw_ref[...]
    y = w[2:3, :] * u + w[1:2, :] * ext_ref[7:7 + rows, :] + w[0:1, :] * ext_ref[6:6 + rows, :]
    mixed_ref[:, 0:MIX_HALF] = (bg_ref[...] * y).astype(BF16)
    ext_ref[0:8, :] = u[rows - 8:rows, :]
    convst_ref[0] = u[rows - (SC_K - 1):rows, :]

    pair = 2 * GLA_DK
    own_block = ((lax.broadcasted_iota(jnp.int32, (pair, 2 * c), 0) < GLA_DK)
                 == (lax.broadcasted_iota(jnp.int32, (pair, 2 * c), 1) < c))

    def pair_dot(qp, ktp):
        kbd = jnp.where(own_block, jnp.concatenate([ktp, ktp], axis=1), 0.0).astype(BF16)
        return _dot(qp.astype(BF16), kbd)

    chunks = []
    for n in range(rows // c):
        rs = slice(n * c, (n + 1) * c)
        z = _dot(r_ref[rs, :].astype(BF16), w2_ref[...]) + bg2_ref[...]
        la = _log_sigmoid(z) * GLA_INV_TAU
        zt = _dot(w2t_ref[...], rt_ref[0, :, rs].astype(BF16)) + bg2c_ref[...]
        lat = _log_sigmoid(zt) * GLA_INV_TAU
        hi, lo = _split_hi_lo(la)
        eq = _dot(mq_ref[...], jnp.concatenate([hi, lo], axis=0))
        hit, lot = _split_hi_lo(lat)
        ek = _dot(jnp.concatenate([hit, lot], axis=1), mk_ref[...])
        chunks.append((rs, q_ref[rs, :] * (GLA_DK ** -0.5), kt_ref[0, :, rs], eq, ek))

    all_scores = []
    for rs, q, kt, eq, ek in chunks:
        scores = []
        for hp in range(GLA_HEADS // 2):
            ps = slice(hp * pair, (hp + 1) * pair)
            qp = q[:, ps]
            ktp = kt[ps, :]
            a = lvl_ref[0] * pair_dot(qp, ktp)
            for l in range(1, GLA_LEVELS + 1):
                ql = qp * jnp.exp(eq[l * c:(l + 1) * c, ps])
                kl = ktp * jnp.exp(ek[ps, l * c:(l + 1) * c])
                a = a + lvl_ref[l] * pair_dot(ql, kl)
            a = a.astype(BF16)
            scores += [a[:, 0:c], a[:, c:2 * c]]
        all_scores.append(scores)

    for (rs, q, kt, eq, ek), scores in zip(chunks, all_scores):
        for h in range(GLA_HEADS):
            ks = slice(h * GLA_DK, (h + 1) * GLA_DK)
            vs = slice(h * GLA_DV, (h + 1) * GLA_DV)
            qh = q[:, ks]
            kth = kt[ks, :]
            vh = v_ref[rs, vs].astype(BF16)
            s_old = s_ref[h]
            q_read = (qh * jnp.exp(eq[0:c, ks])).astype(BF16)
            o = _dot(scores[h], vh) + _dot(q_read, s_old.astype(BF16))
            k_write = (kth * jnp.exp(ek[ks, 0:c])).astype(BF16)
            a_chunk = jnp.exp(ek[ks, (GLA_LEVELS + 1) * c:(GLA_LEVELS + 2) * c])
            s_ref[h] = a_chunk * s_old + _dot(k_write, vh)
            on = o * lax.rsqrt(jnp.mean(o * o, axis=-1, keepdims=True) + EPS) * ng_ref[...]
            mixed_ref[rs, MIX_HALF + h * GLA_DV:MIX_HALF + (h + 1) * GLA_DV] = (
                on * _silu(gt_ref[rs, vs])).astype(BF16)
    glast_ref[0] = s_ref[...]


def _mix0(pn, kt, rt, conv_w, w2, w2t, bg2, bg2c, ng, nb):
    m = pn.shape[0]
    l = m // nb
    c = min(l, MIX_ROWS)
    assert c % GLA_CHUNK == 0 and l % c == 0
    nl = l // c
    mq, mk = _gla_sum_matrices()
    lvl = _gla_level_masks()
    row = lambda b, i: b * nl + i
    full = lambda a: pl.BlockSpec(a.shape, lambda b, i: (0,) * a.ndim)
    wide = lambda col: pl.BlockSpec((c, MIX_HALF), lambda b, i: (row(b, i), col))
    q_col = 5 * MIX_HALF // (GLA_HEADS * GLA_DK)
    r_col = (5 * MIX_HALF + GLA_HEADS * GLA_DK) // R_PAD
    return pl.pallas_call(
        _mix0_kernel,
        out_shape=[jax.ShapeDtypeStruct((m, D_MODEL), BF16),
                   jax.ShapeDtypeStruct((nb, SC_K - 1, MIX_HALF), F32),
                   jax.ShapeDtypeStruct((nb, GLA_HEADS, GLA_DK, GLA_DV), F32)],
        grid=(nb, nl),
        in_specs=[wide(0), wide(1), wide(2), wide(3), wide(4),
                  pl.BlockSpec((c, GLA_HEADS * GLA_DK), lambda b, i: (row(b, i), q_col)),
                  pl.BlockSpec((c, R_PAD), lambda b, i: (row(b, i), r_col)),
                  pl.BlockSpec((1, GLA_HEADS * GLA_DK, c), lambda b, i: (b, 0, i)),
                  pl.BlockSpec((1, GLA_RANK, c), lambda b, i: (b, 0, i)),
                  full(conv_w), full(w2), full(w2t), full(bg2), full(bg2c), full(ng), full(mq), full(mk),
                  full(lvl)],
        out_specs=[pl.BlockSpec((c, D_MODEL), lambda b, i: (row(b, i), 0)),
                   pl.BlockSpec((1, SC_K - 1, MIX_HALF), lambda b, i: (b, 0, 0)),
                   pl.BlockSpec((1, GLA_HEADS, GLA_DK, GLA_DV), lambda b, i: (b, 0, 0, 0))],
        scratch_shapes=[pltpu.VMEM((GLA_HEADS, GLA_DK, GLA_DV), F32), pltpu.VMEM((c + 8, MIX_HALF), F32)],
        compiler_params=_cparams("parallel", "arbitrary"),
        name="mix0",
    )(pn, pn, pn, pn, pn, pn, pn, kt, rt, conv_w, w2, w2t, bg2, bg2c, ng, mq, mk, lvl)


C_ROWS = 16
LOG2E = 1.4426950408889634


def _fox_c_kernel(ft_ref, bf_ref, lf_ref, c_ref):
    lf = _log_sigmoid(ft_ref[0] + bf_ref[...])
    lf_ref[0] = lf
    n = lf.shape[1]
    lane = lax.broadcasted_iota(jnp.int32, lf.shape, 1)
    suf = lf
    s = 1
    while s < n:
        suf = suf + jnp.where(lane + s < n, pltpu.roll(suf, n - s, 1), 0.0)
        s *= 2
    bias = (suf - lf) * LOG2E
    hi = bias.astype(BF16).astype(F32)
    r1 = bias - hi
    mid = r1.astype(BF16).astype(F32)
    lo = (r1 - mid).astype(BF16).astype(F32)
    row = lax.broadcasted_iota(jnp.int32, (C_ROWS, n), 0)
    for h in range(FOX_HEADS):
        hs = slice(h, h + 1)
        parts = jnp.where(row == 0, hi[hs], jnp.where(row == 1, mid[hs], jnp.where(row == 2, lo[hs], 0.0)))
        c_ref[0, h] = parts.astype(BF16)


def _fox_c(ft, bf_col):
    nb, _, l = ft.shape
    spec = pl.BlockSpec((1, FOX_HEADS, l), lambda b: (b, 0, 0))
    return pl.pallas_call(
        _fox_c_kernel,
        out_shape=[jax.ShapeDtypeStruct((nb, FOX_HEADS, l), F32),
                   jax.ShapeDtypeStruct((nb, FOX_HEADS, C_ROWS, l), BF16)],
        grid=(nb,),
        in_specs=[spec, pl.BlockSpec((FOX_HEADS, 1), lambda b: (0, 0))],
        out_specs=[spec, pl.BlockSpec((1, FOX_HEADS, C_ROWS, l), lambda b: (b, 0, 0, 0))],
        compiler_params=_cparams("parallel"),
        name="fox_c",
    )(ft, bf_col)


AUG = 2 * FOX_HD
FLASH_AHEAD = 8


def _fox_flash_kernel(dec_pages, steps_per_seq, seq0, n_seq, *refs):
    sched_ref = refs[0]
    if dec_pages:
        pt_ref = refs[1]
        refs = refs[2:]
    else:
        refs = refs[1:]
    q_ref, kt_ref, vt_ref, c_ref = refs[:4]
    refs = refs[4:]
    t = pl.program_id(0)
    if dec_pages:
        dec_in, refs = refs[:N_DEC_IN], refs[N_DEC_IN:]
        o_ref, do_ref, dlf_ref, qa_ref, m_ref, acc_ref = refs[:6]
        seq_of = lambda s: seq0 + jnp.minimum(s // steps_per_seq, n_seq - 1)
        dec, drain = _decode_begin(pt_ref, dec_in, (do_ref, dlf_ref), refs[6:], t, pl.num_programs(0),
                                   seq_of, steps_per_seq, dec_pages)
        dacc_ref = dec[13]
    else:
        o_ref, qa_ref, m_ref, acc_ref = refs
    qi = sched_ref[1, t]
    kj = sched_ref[2, t]
    tq = q_ref.shape[0]
    tk = kt_ref.shape[2]
    lane = lax.broadcasted_iota(jnp.int32, (tq, AUG), 1)
    if dec_pages:
        dstep = t % steps_per_seq
        _decode_init(dstep, *dec)

    @pl.when(kj == 0)
    def _():
        ones3 = jnp.where(lane < FOX_HD + 3, 1.0, 0.0)
        for p in range(FOX_HEADS // 2):
            qp = q_ref[:, p * AUG:(p + 1) * AUG] * (FOX_HD ** -0.5 * LOG2E)
            qa_ref[2 * p] = jnp.where(lane < FOX_HD, qp, ones3).astype(BF16)
            qa_ref[2 * p + 1] = jnp.where(lane < FOX_HD, pltpu.roll(qp, FOX_HD, 1), ones3).astype(BF16)
        m_ref[...] = jnp.full_like(m_ref, -jnp.inf)
        acc_ref[...] = jnp.zeros_like(acc_ref)

    def block(masked):
        if masked:
            visible = (lax.broadcasted_iota(jnp.int32, (tq, tk), 1) <= lax.broadcasted_iota(jnp.int32, (tq, tk), 0))
        pad_k = jnp.zeros((AUG - FOX_HD - C_ROWS, tk), BF16)
        ones_row = jnp.where(lax.broadcasted_iota(jnp.int32, (AUG - FOX_HD, tk), 0) == 0, 1.0, 0.0).astype(BF16)

        def logits(h):
            hs = slice(h * FOX_HD, (h + 1) * FOX_HD)
            ka = jnp.concatenate([kt_ref[0, hs, :], c_ref[0, h], pad_k], axis=0)
            return _dot(qa_ref[h], ka)

        s_ahead = [logits(h) for h in range(FLASH_AHEAD)]
        if dec_pages:
            alpha_d, p_d = _decode_logits(*dec)
        for h in range(FOX_HEADS):
            if dec_pages and h == FOX_HEADS // 2:
                _decode_values(alpha_d, p_d, dec[6], dacc_ref)
            if h + FLASH_AHEAD < FOX_HEADS:
                s_ahead.append(logits(h + FLASH_AHEAD))
            s = s_ahead.pop(0)
            hs = slice(h * FOX_HD, (h + 1) * FOX_HD)
            va = jnp.concatenate([vt_ref[0, hs, :], ones_row], axis=0)
            if masked:
                s = jnp.where(visible, s, -jnp.inf)
            m_old = m_ref[h]
            m_new = jnp.maximum(m_old, jnp.max(s, axis=-1, keepdims=True))
            pr = jnp.exp2(s - jnp.concatenate([m_new] * (tk // AUG), axis=1)).astype(BF16)
            acc_ref[h] = jnp.exp2(m_old - m_new) * acc_ref[h] + _dot_nt(pr, va)
            m_ref[h] = m_new

    @pl.when(kj < qi)
    def _():
        block(False)

    @pl.when(kj == qi)
    def _():
        block(True)
        for p in range(FOX_HEADS // 2):
            a0 = acc_ref[2 * p]
            a1 = acc_ref[2 * p + 1]
            o0 = a0 * (1.0 / a0[:, FOX_HD:FOX_HD + 1])
            o1 = a1 * (1.0 / a1[:, FOX_HD:FOX_HD + 1])
            o_ref[:, p * AUG:(p + 1) * AUG] = jnp.where(lane < FOX_HD, o0, pltpu.roll(o1, FOX_HD, 1)).astype(BF16)

    if dec_pages:
        _decode_finish(dstep, steps_per_seq, do_ref, dec[11], dacc_ref)
        drain()


def _fox_flash(pn, kt, vt, caug, nb, tq, dec=None):
    m = pn.shape[0]
    l = m // nb
    nq = l // tq
    sched = jnp.asarray(np.array([(b, i, j) for b in range(nb) for i in range(nq) for j in range(i + 1)],
                                 np.int32).T)
    steps = sched.shape[1]
    kv_spec = pl.BlockSpec((1, MIX_HALF, tq), lambda t, sc, *_: (sc[0, t], 0, sc[2, t]))
    q_map = lambda t, sc, *_: (sc[0, t] * nq + sc[1, t], 0)
    in_specs = [pl.BlockSpec((tq, MIX_HALF), q_map), kv_spec, kv_spec,
                pl.BlockSpec((1, FOX_HEADS, C_ROWS, tq), lambda t, sc, *_: (sc[0, t], 0, 0, sc[2, t]))]
    out_shape = [jax.ShapeDtypeStruct((m, MIX_HALF), BF16)]
    out_specs = [pl.BlockSpec((tq, MIX_HALF), q_map)]
    scratch = [pltpu.VMEM((FOX_HEADS, tq, AUG), BF16), pltpu.VMEM((FOX_HEADS, tq, AUG), F32),
               pltpu.VMEM((FOX_HEADS, tq, AUG), F32)]
    if dec is None:
        grid_spec = pltpu.PrefetchScalarGridSpec(
            num_scalar_prefetch=1, grid=(steps,), in_specs=in_specs, out_specs=out_specs, scratch_shapes=scratch)
        return pl.pallas_call(
            functools.partial(_fox_flash_kernel, 0, 0, 0, 0), out_shape=out_shape, grid_spec=grid_spec,
            compiler_params=_cparams("arbitrary"), name="fox_flash")(sched, pn, kt, vt, caug)[0]
    page_table, small, (ck, cv, clf), seq0, n_seq, g = dec
    n_pages = page_table.shape[1]
    sps = n_pages // g
    slots = steps // sps
    assert steps % sps == 0 and slots >= n_seq
    seq_of = lambda t, *_: seq0 + jnp.minimum(t // sps, n_seq - 1)
    dec_specs, dec_scratch = _decode_specs(small, seq_of, ck.shape[3], g)
    d_shapes, d_specs = _decode_outs(slots, lambda t, *_: t // sps)
    grid_spec = pltpu.PrefetchScalarGridSpec(
        num_scalar_prefetch=2, grid=(steps,), in_specs=in_specs + dec_specs, out_specs=out_specs + d_specs,
        scratch_shapes=scratch + dec_scratch)
    o_c, o_part, lf_part = pl.pallas_call(
        functools.partial(_fox_flash_kernel, g, sps, seq0, n_seq), out_shape=out_shape + d_shapes,
        grid_spec=grid_spec, compiler_params=_cparams("arbitrary"), name="fox_flash_decode",
    )(sched, page_table, pn, kt, vt, caug, *small, ck, cv, clf)
    return o_c, o_part[:n_seq], lf_part[:n_seq]


CF_HALO = 32
SUBLANES = 8


CONF_ROWS = 256


def _conf_rows(a, gate, w_ref, cb_ref, g_ref, b_ref, ext_ref):
    t = a.shape[0]
    u = a * _sigmoid(gate)
    ext_ref[CF_HALO:CF_HALO + t, :] = u
    off = CF_HALO - (CF_K - 1)
    acc = jnp.zeros((t, MIX_HALF), F32) + cb_ref[...]
    for rho in range(SUBLANES):
        taps = [j for j in range(CF_K) if (off + j) % SUBLANES == rho]
        rows = t if rho == 0 else t + SUBLANES
        part = jnp.zeros((rows, MIX_HALF), F32)
        for j in taps:
            a0 = off + j - rho
            part = part + w_ref[j:j + 1, :] * ext_ref[a0:a0 + rows, :]
        acc = acc + part[rho:rho + t, :]
    ext_ref[0:CF_HALO, :] = ext_ref[t:t + CF_HALO, :]
    wide = lambda r: jnp.concatenate([r] * (MIX_HALF // LANES), axis=1)
    d = acc - wide(_lane_sum_mxu_wide(acc) * (1.0 / MIX_HALF))
    inv = lax.rsqrt(_lane_sum_mxu_wide(d * d) * (1.0 / MIX_HALF) + EPS)
    return _silu(d * wide(inv) * g_ref[...] + b_ref[...]).astype(BF16)


def _proj_conf_kernel(t_outs, x_ref, wn_ref, wt_ref, w_ref, cb_ref, g_ref, b_ref, q_ref, y_ref, st_ref, *rest):
    pt_refs, ext_ref = rest[:-1], rest[-1]
    tl = x_ref.shape[0]
    sub = min(tl, CONF_ROWS)

    @pl.when(pl.program_id(1) == 0)
    def _():
        ext_ref[0:CF_HALO, :] = jnp.zeros((CF_HALO, MIX_HALF), F32)

    xb = x_ref[...].astype(BF16)
    n_groups = tl // sub
    group = lambda n: _dot(xb[n * sub:(n + 1) * sub], wn_ref[...])
    pn_next = group(0)
    pt = None
    for n in range(n_groups):
        pn = pn_next
        if n + 1 < n_groups:
            pn_next = group(n + 1)
        else:
            pt = _dot_nt(wt_ref[...], xb)
        rs = slice(n * sub, (n + 1) * sub)
        q_ref[rs, :] = pn[:, 0:MIX_HALF]
        y_ref[rs, :] = _conf_rows(pn[:, MIX_HALF:2 * MIX_HALF], pn[:, 2 * MIX_HALF:3 * MIX_HALF],
                                  w_ref, cb_ref, g_ref, b_ref, ext_ref)
    st_ref[0] = ext_ref[CF_HALO - (CF_K - 1):CF_HALO, :]
    for ref, (r0, n, dt) in zip(pt_refs, t_outs):
        ref[0] = pt[r0:r0 + n].astype(dt)


def _proj_conf(x2d, wn, wt, t_outs, w, cb, g, b, nb, tl):
    m, k = x2d.shape
    l = m // nb
    nl = l // tl
    full = lambda a: pl.BlockSpec(a.shape, lambda bb, i: (0,) * a.ndim)
    rows = lambda bb, i: (bb * nl + i, 0)
    out_shape = [jax.ShapeDtypeStruct((m, MIX_HALF), F32), jax.ShapeDtypeStruct((m, MIX_HALF), BF16),
                 jax.ShapeDtypeStruct((nb, CF_K - 1, MIX_HALF), F32)]
    out_shape += [jax.ShapeDtypeStruct((nb, n, l), dt) for _, n, dt in t_outs]
    out_specs = [pl.BlockSpec((tl, MIX_HALF), rows), pl.BlockSpec((tl, MIX_HALF), rows),
                 pl.BlockSpec((1, CF_K - 1, MIX_HALF), lambda bb, i: (bb, 0, 0))]
    out_specs += [pl.BlockSpec((1, n, tl), lambda bb, i: (bb, 0, i)) for _, n, _ in t_outs]
    return pl.pallas_call(
        functools.partial(_proj_conf_kernel, t_outs),
        out_shape=out_shape,
        grid=(nb, nl),
        in_specs=[pl.BlockSpec((tl, k), rows), full(wn), full(wt), full(w), full(cb), full(g), full(b)],
        out_specs=out_specs,
        scratch_shapes=[pltpu.VMEM((min(tl, CONF_ROWS) + CF_HALO, MIX_HALF), F32)],
        compiler_params=_cparams("parallel", "arbitrary"),
        name="proj_conf",
    )(x2d, wn, wt, w, cb, g, b)


def _smix0_kernel(pn_ref, kc_ref, qc_ref, rc_ref, prev_ref, s_ref, convw_ref, w2t_ref, bg2c_ref, ng_ref,
                  mixed_ref, convst_ref, sout_ref):
    w = convw_ref[...]
    w2t = w2t_ref[...]
    for i in range(pn_ref.shape[0]):
        pn = pn_ref[i]
        bg, cg, hv = pn[:, 0:MIX_HALF], pn[:, MIX_HALF:2 * MIX_HALF], pn[:, 2 * MIX_HALF:3 * MIX_HALF]
        v, gt = pn[:, 3 * MIX_HALF:4 * MIX_HALF], pn[:, 4 * MIX_HALF:5 * MIX_HALF]
        u = cg * hv
        prev = prev_ref[i]
        y = w[0:1, :] * prev[0:1, :] + w[1:2, :] * prev[1:2, :] + w[2:3, :] * u
        mixed_ref[i, :, 0:MIX_HALF] = (bg * y).astype(BF16)
        convst_ref[i, 0:1, :] = prev[1:2, :]
        convst_ref[i, 1:2, :] = u

        zc = _dot(w2t, rc_ref[i].astype(BF16)) + bg2c_ref[...]
        ac = jnp.exp(_log_sigmoid(zc) * GLA_INV_TAU)
        qc = qc_ref[i] * (GLA_DK ** -0.5)
        kc = kc_ref[i]
        for h in range(GLA_HEADS):
            ks = slice(h * GLA_DK, (h + 1) * GLA_DK)
            vs = slice(h * GLA_DV, (h + 1) * GLA_DV)
            s_new = ac[ks, :] * s_ref[i, h] + kc[ks, :] * v[:, vs]
            sout_ref[i, h] = s_new
            o = jnp.sum(qc[ks, :] * s_new, axis=0, keepdims=True)
            on = o * lax.rsqrt(jnp.mean(o * o, axis=-1, keepdims=True) + EPS) * ng_ref[...]
            mixed_ref[i, :, MIX_HALF + h * GLA_DV:MIX_HALF + (h + 1) * GLA_DV] = (
                on * _silu(gt[:, vs])).astype(BF16)


def _smix0(pn3, kc, qc, rc, prev, s0, conv_w, w2t, bg2c, ng):
    nd, _, nn = pn3.shape
    sb = _tile_rows(nd, SAMPLE_SEQS_PER_STEP)
    per = lambda a: pl.BlockSpec((sb,) + a.shape[1:], lambda b: (b,) + (0,) * (a.ndim - 1))
    full = lambda a: pl.BlockSpec(a.shape, lambda b: (0,) * a.ndim)
    out_shape = [jax.ShapeDtypeStruct((nd, 1, D_MODEL), BF16),
                 jax.ShapeDtypeStruct((nd, SC_K - 1, MIX_HALF), F32),
                 jax.ShapeDtypeStruct(s0.shape, F32)]
    return pl.pallas_call(
        _smix0_kernel,
        out_shape=out_shape,
        grid=(nd // sb,),
        in_specs=[per(pn3), per(kc), per(qc), per(rc), per(prev), per(s0),
                  full(conv_w), full(w2t), full(bg2c), full(ng)],
        out_specs=[per(o) for o in out_shape],
        compiler_params=_cparams("parallel"),
        name="smix0",
    )(pn3, kc, qc, rc, prev, s0, conv_w, w2t, bg2c, ng)


def _sconf_kernel(pn_ref, prev_ref, w_ref, cb_ref, g_ref, b_ref, y_ref, st_ref):
    w = w_ref[...]
    for i in range(pn_ref.shape[0]):
        pn = pn_ref[i]
        a, gate = pn[:, MIX_HALF:2 * MIX_HALF], pn[:, 2 * MIX_HALF:3 * MIX_HALF]
        u = a * _sigmoid(gate)
        prev = prev_ref[i]
        acc = jnp.sum(w[0:CF_K - 1, :] * prev, axis=0, keepdims=True) + w[CF_K - 1:CF_K, :] * u + cb_ref[...]
        y_ref[i] = _silu(_layernorm(acc, g_ref[...], b_ref[...])).astype(BF16)
        st_ref[i, 0:CF_K - 2, :] = prev_ref[i, 1:CF_K - 1, :]
        st_ref[i, CF_K - 2:CF_K - 1, :] = u


def _sconf(pn3, prev, w, cb, g, b):
    nd = pn3.shape[0]
    sb = _tile_rows(nd, SAMPLE_SEQS_PER_STEP)
    per = lambda a: pl.BlockSpec((sb,) + a.shape[1:], lambda bb: (bb,) + (0,) * (a.ndim - 1))
    full = lambda a: pl.BlockSpec(a.shape, lambda bb: (0,) * a.ndim)
    out_shape = [jax.ShapeDtypeStruct((nd, 1, MIX_HALF), BF16), jax.ShapeDtypeStruct(prev.shape, F32)]
    return pl.pallas_call(
        _sconf_kernel,
        out_shape=out_shape,
        grid=(nd // sb,),
        in_specs=[per(pn3), per(prev), full(w), full(cb), full(g), full(b)],
        out_specs=[per(o) for o in out_shape],
        compiler_params=_cparams("parallel"),
        name="sconf",
    )(pn3, prev, w, cb, g, b)


def _decode_kernel(seq0, steps_per_seq, pt_ref, *refs):
    u = pl.program_id(0)
    step = u % steps_per_seq
    dec, drain = _decode_begin(pt_ref, refs[:N_DEC_IN], refs[N_DEC_IN:N_DEC_IN + 2], refs[N_DEC_IN + 2:], u,
                               pl.num_programs(0), lambda s: seq0 + s // steps_per_seq, steps_per_seq,
                               PAGES_PER_STEP)
    _decode_init(step, *dec)
    alpha, p_pages = _decode_logits(*dec)
    _decode_values(alpha, p_pages, dec[6], dec[13])
    _decode_finish(step, steps_per_seq, dec[8], dec[11], dec[13])
    drain()


def _decode(page_table, small, caches, seq0, n_seq):
    ck, cv, clf = caches
    g = PAGES_PER_STEP
    sps = page_table.shape[1] // g
    in_specs, scratch = _decode_specs(small, lambda u: seq0 + u // sps, ck.shape[3], g)
    out_shape, out_specs = _decode_outs(n_seq, lambda u: u // sps)
    grid_spec = pltpu.PrefetchScalarGridSpec(
        num_scalar_prefetch=1, grid=(n_seq * sps,), in_specs=in_specs, out_specs=out_specs,
        scratch_shapes=scratch)
    return pl.pallas_call(
        functools.partial(_decode_kernel, seq0, sps),
        out_shape=out_shape,
        grid_spec=grid_spec,
        compiler_params=_cparams("arbitrary"),
        name="fox_decode",
    )(page_table, *small, ck, cv, clf)


def _tile_rows(n, pref):
    t = min(n, pref)
    while n % t:
        t //= 2
    return t


def kernel(x_prompt, x_sample, state_conv_a, state_gla, cache_k, cache_v, cache_logf, state_conv_d, page_table,
           ab_w_in, ab_conv_w, gla_w_gate2, gla_b_gate, gla_norm_g, ab_w_out,
           cd_w_in, fox_b_f, cf_conv_w, cf_conv_b, cf_ln_g, cf_ln_b, cd_w_out,
           ffn_w_in, ffn_w_out, ln_g, ln_b):
    nb, seq, d = x_prompt.shape
    nd = x_sample.shape[0]
    hk = GLA_HEADS * GLA_DK

    o_bg, o_cg, o_hv = 0, MIX_HALF, 2 * MIX_HALF
    o_q = 3 * MIX_HALF
    o_k = o_q + hk
    o_v = o_k + hk
    o_g = o_v + MIX_HALF
    o_r = o_g + MIX_HALF
    wab = ab_w_in.astype(BF16)
    ab_wn = jnp.concatenate([wab[:, o_bg:o_q], wab[:, o_v:o_g], wab[:, o_g:o_r], wab[:, o_q:o_k],
                             jnp.pad(wab[:, o_r:o_r + GLA_RANK], ((0, 0), (0, R_PAD - GLA_RANK)))], axis=1)
    wab_t = wab.T
    ab_wt_p = jnp.concatenate([wab_t[o_k:o_v], wab_t[o_r:o_r + GLA_RANK]], axis=0)
    ab_wt_s = jnp.concatenate([ab_wt_p, wab_t[o_q:o_k]], axis=0)
    w2 = jnp.pad(gla_w_gate2.astype(BF16), ((0, R_PAD - GLA_RANK), (0, 0)))
    w2t = gla_w_gate2.T.astype(BF16)
    bg2 = gla_b_gate.reshape(1, hk)
    bg2c = gla_b_gate.reshape(hk, 1)
    ng = gla_norm_g.reshape(1, GLA_DV)
    ab_wo = ab_w_out.astype(BF16)

    wcd = cd_w_in.astype(BF16)
    c_q, c_k, c_v = 0, MIX_HALF, 2 * MIX_HALF
    c_f = 3 * MIX_HALF
    c_a = c_f + FOX_HEADS
    c_gate = c_a + MIX_HALF
    cd_wn = jnp.concatenate([wcd[:, c_q:c_k], wcd[:, c_a:c_gate], wcd[:, c_gate:c_gate + MIX_HALF]], axis=1)
    cd_wn_s = jnp.concatenate([cd_wn, wcd[:, c_k:c_f]], axis=1)
    wcd_t = wcd.T
    cd_wt_p = jnp.concatenate([wcd_t[c_k:c_v], wcd_t[c_v:c_f],
                               jnp.pad(wcd_t[c_f:c_a], ((0, F_PAD - FOX_HEADS), (0, 0)))], axis=0)
    cd_wt_s = jnp.concatenate([cd_wt_p, wcd_t[c_q:c_k]], axis=0)
    bf_col = fox_b_f.reshape(FOX_HEADS, 1)
    cb = cf_conv_b.reshape(1, MIX_HALF)
    cg_ = cf_ln_g.reshape(1, MIX_HALF)
    cbb = cf_ln_b.reshape(1, MIX_HALF)
    cd_wo = cd_w_out.astype(BF16)
    ffn_wi = ffn_w_in.astype(BF16)
    ffn_wo = ffn_w_out.astype(BF16)

    def tail(x2d, h1, c1, h2, c2, wo, layer, tm, dec=None):
        return _tail(h1, c1, h2, c2, wo, x2d, ln_g[layer, 0:1], ln_b[layer, 0:1], ffn_wi, ffn_wo, layer,
                     ln_g[layer, 1:2], ln_b[layer, 1:2], tm, dec)

    xs = x_sample.reshape(nd, d)
    pn_s, kc_s, rc_s, qc_s = _proj(xs, ab_wn, ab_wt_s,
                                   ((0, hk, F32), (hk, GLA_RANK, F32), (hk + GLA_RANK, hk, F32)), 1, nd,
                                   as_columns=True)
    mixed_s, conv_a_s, gla_s = _smix0(pn_s.reshape(nd, 1, -1), kc_s, qc_s, rc_s,
                                      state_conv_a, state_gla, ab_conv_w, w2t, bg2c, ng)
    mixed_s = mixed_s.reshape(nd, d)
    xs2 = tail(xs, mixed_s, 0, mixed_s, 1, ab_wo, 0, nd)
    kvf = ((0, MIX_HALF, F32), (MIX_HALF, MIX_HALF, F32), (2 * MIX_HALF, F_PAD, F32))
    cols_s = ((0, MIX_HALF, F32), (MIX_HALF, MIX_HALF, F32), (2 * MIX_HALF, FOX_HEADS, F32),
              (2 * MIX_HALF + F_PAD, MIX_HALF, F32))
    pn1_s, kc1_s, vc1_s, fc1_s, qc1_s = _proj(xs2, cd_wn_s, cd_wt_s, cols_s, 1, nd, as_columns=True)
    caches = (jnp.transpose(cache_k, (0, 2, 3, 1)), jnp.transpose(cache_v, (0, 2, 3, 1)),
              jnp.transpose(cache_logf, (0, 2, 1)))
    dec_small = (qc1_s, kc1_s, vc1_s, fc1_s, bf_col)

    m = nb * seq
    tm = _tile_rows(m, TAIL_ROWS)
    tl = _tile_rows(seq, SEQ_TILE)
    n_pages = page_table.shape[1]
    per_tail = 0
    if n_pages % PAGES_PER_STEP == 0 and (m // tm) % (n_pages // PAGES_PER_STEP) == 0:
        per_tail = (m // tm) // (n_pages // PAGES_PER_STEP)
        if DEPTH * per_tail > nd:
            per_tail = 0
    in_tails = DEPTH * per_tail
    flash_steps = nb * (seq // tl) * (seq // tl + 1) // 2
    in_flash = 0
    if n_pages % FLASH_PAGES == 0 and flash_steps % (n_pages // FLASH_PAGES) == 0:
        in_flash = max(0, min(flash_steps // (n_pages // FLASH_PAGES), nd - in_tails))
    dec_parts = {}

    def prompt_tail(x2d, h1, c1, h2, c2, wo, layer):
        if per_tail == 0:
            return tail(x2d, h1, c1, h2, c2, wo, layer, tm)
        y, o_part, lf_part = tail(x2d, h1, c1, h2, c2, wo, layer, tm,
                                  (page_table, dec_small, caches, layer * per_tail, per_tail))
        dec_parts[layer * per_tail] = (o_part, lf_part)
        return y

    x0 = x_prompt.reshape(m, d)
    pn, kt, rt = _proj(x0, ab_wn, ab_wt_p, ((0, hk, F32), (hk, GLA_RANK, F32)), nb, tl)
    mixed, conv_a_p, gla_p = _mix0(pn, kt, rt, ab_conv_w, w2, w2t, bg2, bg2c, ng, nb)
    x2 = prompt_tail(x0, mixed, 0, mixed, 1, ab_wo, 0)

    pn1, y_d, conv_d_p, kt1, vt1, ft1, kt16, vt16 = _proj_conf(
        x2, cd_wn, cd_wt_p, kvf + ((0, MIX_HALF, BF16), (MIX_HALF, MIX_HALF, BF16)),
        cf_conv_w, cb, cg_, cbb, nb, tl)
    lft, caug = _fox_c(ft1, bf_col)
    if in_flash:
        o_c, o_part, lf_part = _fox_flash(pn1, kt16, vt16, caug, nb, tl,
                                          (page_table, dec_small, caches, in_tails, in_flash, FLASH_PAGES))
        dec_parts[in_tails] = (o_part, lf_part)
    else:
        o_c = _fox_flash(pn1, kt16, vt16, caug, nb, tl)
    y_p = prompt_tail(x2, o_c, 0, y_d, 0, cd_wo, 1)

    y_prompt = y_p.reshape(nb, seq, d)
    k_p = jnp.transpose(kt1.reshape(nb, FOX_HEADS, FOX_HD, seq), (0, 3, 1, 2))
    v_p = jnp.transpose(vt1.reshape(nb, FOX_HEADS, FOX_HD, seq), (0, 3, 1, 2))
    lf_p = jnp.transpose(lft, (0, 2, 1))

    done = in_tails + in_flash
    if done < nd:
        dec_parts[done] = tuple(_decode(page_table, dec_small, caches, done, nd - done))
    o_col = jnp.concatenate([dec_parts[k][0] for k in sorted(dec_parts)], axis=0)
    lf_col = jnp.concatenate([dec_parts[k][1] for k in sorted(dec_parts)], axis=0)
    o_cs = o_col.reshape(nd, MIX_HALF).astype(BF16)
    y_ds, conv_d_s = _sconf(pn1_s.reshape(nd, 1, -1), state_conv_d, cf_conv_w, cb, cg_, cbb)
    y_s = tail(xs2, o_cs, 0, y_ds.reshape(nd, MIX_HALF), 0, cd_wo, 1, nd)

    y_sample = y_s.reshape(nd, 1, d)
    k_s = pn1_s[:, 3 * MIX_HALF:4 * MIX_HALF].reshape(nd, 1, FOX_HEADS, FOX_HD)
    v_s = pn1_s[:, 4 * MIX_HALF:5 * MIX_HALF].reshape(nd, 1, FOX_HEADS, FOX_HD)
    lf_s = lf_col.reshape(nd, 1, FOX_HEADS)

    return (y_prompt, y_sample, conv_a_p, conv_a_s, gla_p, gla_s, k_p, k_s, v_p, v_s, lf_p, lf_s,
            conv_d_p, conv_d_s)
```

```python
import functools

import numpy as np
import jax
import jax.numpy as jnp
from jax import lax
from jax.experimental import pallas as pl
from jax.experimental.pallas import tpu as pltpu

F32 = jnp.float32
BF16 = jnp.bfloat16

D_MODEL = 1024
MIX_HALF = D_MODEL // 2
SC_K = 3
GLA_HEADS = 4
GLA_DV = MIX_HALF // GLA_HEADS
GLA_DK = GLA_DV // 2
GLA_RANK = 16
GLA_INV_TAU = 1.0 / 16.0
GLA_CHUNK = 128
GLA_LEVELS = 7
MIX_ROWS = 512
FOX_HEADS = 8
FOX_HD = MIX_HALF // FOX_HEADS
CF_K = 31
D_FF = 2816
DEPTH = 2
DN_ALPHA = (2 * DEPTH) ** 0.25
EPS = 1e-5
R_PAD = 128
F_PAD = 16
VMEM_LIMIT = 56 * 1024 * 1024
PAGES_PER_STEP = 16
FLASH_PAGES = 16
TAIL_ROWS = 256
SEQ_TILE = 512
SAMPLE_SEQS_PER_STEP = 8


def _cparams(*sem):
    return pltpu.CompilerParams(dimension_semantics=sem, vmem_limit_bytes=VMEM_LIMIT)


def _log_sigmoid(x):
    return jnp.minimum(x, 0.0) - jnp.log1p(jnp.exp(-jnp.abs(x)))


def _sigmoid(x):
    return 1.0 / (1.0 + jnp.exp(-x))


def _silu(x):
    return x * _sigmoid(x)


def _layernorm(y, g, b):
    mu = jnp.mean(y, axis=-1, keepdims=True)
    d = y - mu
    var = jnp.mean(d * d, axis=-1, keepdims=True)
    return d * lax.rsqrt(var + EPS) * g + b


def _dot(a, b):
    return jnp.dot(a, b, preferred_element_type=F32)


def _dot_nt(a, b):
    return lax.dot_general(a, b, (((1,), (1,)), ((), ())), preferred_element_type=F32)


def _proj_kernel(t_outs, as_columns, x_ref, wn_ref, wt_ref, pn_ref, *pt_refs):
    xb = x_ref[...].astype(BF16)
    pn_ref[...] = _dot(xb, wn_ref[...])
    pt = _dot_nt(wt_ref[...], xb)
    for ref, (r0, n, dt) in zip(pt_refs, t_outs):
        if as_columns:
            for row in range(x_ref.shape[0]):
                ref[row] = pt[r0:r0 + n, row:row + 1].astype(dt)
        else:
            ref[0] = pt[r0:r0 + n].astype(dt)


def _proj(x2d, wn, wt, t_outs, nb, tl, as_columns=False):
    m, k = x2d.shape
    l = m // nb
    nl = l // tl
    nn = wn.shape[1]
    nt = wt.shape[0]
    out_shape = [jax.ShapeDtypeStruct((m, nn), F32)]
    out_specs = [pl.BlockSpec((tl, nn), lambda b, i: (b * nl + i, 0))]
    if as_columns:
        assert nb == 1 and nl == 1
        out_shape += [jax.ShapeDtypeStruct((m, n, 1), dt) for _, n, dt in t_outs]
        out_specs += [pl.BlockSpec((m, n, 1), lambda b, i: (0, 0, 0)) for _, n, _ in t_outs]
    else:
        out_shape += [jax.ShapeDtypeStruct((nb, n, l), dt) for _, n, dt in t_outs]
        out_specs += [pl.BlockSpec((1, n, tl), lambda b, i: (b, 0, i)) for _, n, _ in t_outs]
    return pl.pallas_call(
        functools.partial(_proj_kernel, t_outs, as_columns),
        out_shape=out_shape,
        grid=(nb, nl),
        in_specs=[pl.BlockSpec((tl, k), lambda b, i: (b * nl + i, 0)),
                  pl.BlockSpec((k, nn), lambda b, i: (0, 0)),
                  pl.BlockSpec((nt, k), lambda b, i: (0, 0))],
        out_specs=out_specs,
        compiler_params=_cparams("parallel", "arbitrary"),
        name="proj",
    )(x2d, wn, wt)


LANES = 128


def _lane_sum_mxu_wide(x):
    ones = jnp.ones((x.shape[1], LANES), BF16)
    hi, lo = _split_hi_lo(x)
    return _dot(hi, ones) + _dot(lo, ones)


def _lane_sum_mxu(x):
    return _lane_sum_mxu_wide(x)[:, 0:1]


def _per_head_rows(a):
    return jnp.concatenate([jnp.broadcast_to(a[h:h + 1, :], (FOX_HD, 1)) for h in range(FOX_HEADS)], axis=0)


def _decode_init(step, qc_ref, kc_ref, vc_ref, fc_ref, bf_ref, k_refs, v_refs, lf_refs,
                 o_ref, lfo_ref, m_ref, l_ref, r_ref, acc_ref):
    @pl.when(step == 0)
    def _():
        qc = qc_ref[0] * (FOX_HD ** -0.5)
        lf_new = _log_sigmoid(fc_ref[0] + bf_ref[...])
        lfo_ref[0] = lf_new
        r_ref[...] = lf_new
        l_ref[...] = jnp.ones_like(l_ref)
        head_col = lax.broadcasted_iota(jnp.int32, (FOX_HEADS, 1), 0)
        s_self = jnp.zeros((FOX_HEADS, 1), F32)
        for h in range(FOX_HEADS):
            hs = slice(h * FOX_HD, (h + 1) * FOX_HD)
            s_h = jnp.sum(qc[hs, :] * kc_ref[0, hs, :], axis=0, keepdims=True)
            s_self = jnp.where(head_col == h, s_h, s_self)
        m_ref[...] = s_self
        acc_ref[...] = vc_ref[0]


def _decode_logits(qc_ref, kc_ref, vc_ref, fc_ref, bf_ref, k_refs, v_refs, lf_refs,
                   o_ref, lfo_ref, m_ref, l_ref, r_ref, acc_ref):
    rows = k_refs[0].shape[-1]
    qc = qc_ref[0] * (FOX_HD ** -0.5)
    lane = lax.broadcasted_iota(jnp.int32, (FOX_HEADS, rows), 1)
    head_row = lax.broadcasted_iota(jnp.int32, (FOX_HEADS, rows), 0)
    q_wide = [jnp.broadcast_to(qc[h * FOX_HD:(h + 1) * FOX_HD, :], (FOX_HD, rows)) for h in range(FOX_HEADS)]
    r_run = r_ref[...]
    s_pages = []
    for k_ref, lf_ref in zip(k_refs, lf_refs):
        lf = lf_ref[...]
        pre = lf
        sh = 1
        while sh < rows:
            pre = pre + jnp.where(lane >= sh, pltpu.roll(pre, sh, 1), 0.0)
            sh *= 2
        tot = pre[:, rows - 1:rows]
        bias = r_run + (tot - pre)
        r_run = r_run + tot
        s = jnp.zeros((FOX_HEADS, rows), F32)
        for h in range(FOX_HEADS):
            s_h = jnp.sum(q_wide[h] * k_ref[h], axis=0, keepdims=True)
            s = jnp.where(head_row == h, s_h, s)
        s_pages.append(s + bias)
    r_ref[...] = r_run

    m_old = m_ref[...]
    m_new = m_old
    for s in s_pages:
        m_new = jnp.maximum(m_new, jnp.max(s, axis=-1, keepdims=True))
    alpha = jnp.exp(m_old - m_new)
    m_ref[...] = m_new
    p_pages = [jnp.exp(s - m_new) for s in s_pages]
    p_sum = p_pages[0]
    for p in p_pages[1:]:
        p_sum = p_sum + p
    l_ref[...] = alpha * l_ref[...] + _lane_sum_mxu(p_sum)
    return alpha, p_pages


def _decode_values(alpha, p_pages, v_refs, acc_ref):
    parts = []
    for h in range(FOX_HEADS):
        acc = p_pages[0][h:h + 1, :] * v_refs[0][h]
        for p, v_ref in zip(p_pages[1:], v_refs[1:]):
            acc = acc + p[h:h + 1, :] * v_ref[h]
        parts.append(acc)
    acc_ref[...] = _per_head_rows(alpha) * acc_ref[...] + _lane_sum_mxu(jnp.concatenate(parts, axis=0))


def _decode_finish(step, n_steps, o_ref, l_ref, acc_ref):
    @pl.when(step == n_steps - 1)
    def _():
        o_ref[0] = acc_ref[...] * _per_head_rows(1.0 / l_ref[...])


N_DEC_IN = 8
N_DEC_SCRATCH = 8


def _decode_specs(arrs, seq_of, rows, g):
    per = lambda a: pl.BlockSpec((1,) + a.shape[1:], lambda *i: (seq_of(*i[:-1]),) + (0,) * (a.ndim - 1))
    qc, kc, vc, fc, bf_col = arrs
    in_specs = [per(qc), per(kc), per(vc), per(fc), pl.BlockSpec(bf_col.shape, lambda *i: (0, 0))]
    in_specs += [pl.BlockSpec(memory_space=pl.ANY)] * 3
    scratch = [pltpu.VMEM((FOX_HEADS, 1), F32), pltpu.VMEM((FOX_HEADS, 1), F32),
               pltpu.VMEM((FOX_HEADS, 1), F32), pltpu.VMEM((MIX_HALF, 1), F32),
               pltpu.VMEM((2, g, FOX_HEADS, FOX_HD, rows), F32), pltpu.VMEM((2, g, FOX_HEADS, FOX_HD, rows), F32),
               pltpu.VMEM((2, g, FOX_HEADS, rows), F32), pltpu.SemaphoreType.DMA((2, 3))]
    return in_specs, scratch


def _decode_begin(pt_ref, ins, outs, scratch, u, n_steps, seq_of, sps, g):
    m_ref, l_ref, r_ref, acc_ref, kbuf, vbuf, lfbuf, sems = scratch
    ck, cv, clf = ins[5:]
    n_pages = pt_ref.shape[1]

    def copies(step, slot, pages_known=True):
        out = []
        for pg in range(g):
            page = pt_ref[seq_of(step), n_pages - 1 - ((step % sps) * g + pg)] if pages_known else 0
            out.append(pltpu.make_async_copy(ck.at[page], kbuf.at[slot, pg], sems.at[slot, 0]))
            out.append(pltpu.make_async_copy(cv.at[page], vbuf.at[slot, pg], sems.at[slot, 1]))
            out.append(pltpu.make_async_copy(clf.at[page], lfbuf.at[slot, pg], sems.at[slot, 2]))
        return out

    slot = u % 2

    def start_all(cs):
        for n, c in enumerate(cs):
            c.start(priority=n % 2)

    @pl.when(u == 0)
    def _():
        start_all(copies(u, slot))

    start_all(copies(jnp.minimum(u + 1, n_steps - 1), 1 - slot))
    for c in copies(u, slot, pages_known=False):
        c.wait()
    views = tuple([buf.at[slot, pg] for pg in range(g)] for buf in (kbuf, vbuf, lfbuf))
    dec = (*ins[:5], *views, *outs, m_ref, l_ref, r_ref, acc_ref)

    def drain():
        @pl.when(u == n_steps - 1)
        def _():
            for c in copies(u, 1 - slot, pages_known=False):
                c.wait()

    return dec, drain


def _decode_outs(n_seq, seq_of):
    shapes = [jax.ShapeDtypeStruct((n_seq, MIX_HALF, 1), F32), jax.ShapeDtypeStruct((n_seq, FOX_HEADS, 1), F32)]
    specs = [pl.BlockSpec((1, MIX_HALF, 1), lambda *i: (seq_of(*i[:-1]), 0, 0)),
             pl.BlockSpec((1, FOX_HEADS, 1), lambda *i: (seq_of(*i[:-1]), 0, 0))]
    return shapes, specs


def _tail_kernel(seq0, steps_per_seq, *refs):
    n_dec = steps_per_seq
    if n_dec:
        pt_ref, refs = refs[0], refs[1:]
    h1_ref, h2_ref, wo_ref, x_ref, g1_ref, b1_ref, wg_ref, wu_ref, wd_ref, g2_ref, b2_ref = refs[:11]
    refs = refs[11:]
    if n_dec:
        dec_in, refs = refs[:N_DEC_IN], refs[N_DEC_IN:]
    o_ref = refs[0]
    if n_dec:
        u = pl.program_id(0)
        step = u % steps_per_seq
        dec, drain = _decode_begin(pt_ref, dec_in, refs[1:3], refs[3:], u, pl.num_programs(0),
                                   lambda s: seq0 + s // steps_per_seq, steps_per_seq, PAGES_PER_STEP)
        _decode_init(step, *dec)
    half = h1_ref.shape[1]
    mix = _dot(h1_ref[...], wo_ref[0:half, :]) + _dot(h2_ref[...], wo_ref[half:2 * half, :])
    if n_dec:
        alpha, p_pages = _decode_logits(*dec)
    x1 = _layernorm(DN_ALPHA * x_ref[...] + mix, g1_ref[...], b1_ref[...])
    xb = x1.astype(BF16)
    gate = _dot(xb, wg_ref[...])
    up = _dot(xb, wu_ref[...])
    if n_dec:
        _decode_values(alpha, p_pages, dec[6], dec[13])
    h = (_silu(gate) * up).astype(BF16)
    o_ref[...] = _layernorm(DN_ALPHA * x1 + _dot(h, wd_ref[...]), g2_ref[...], b2_ref[...])
    if n_dec:
        _decode_finish(step, steps_per_seq, dec[8], dec[11], dec[13])
        drain()


def _tail(h1, h1_col, h2, h2_col, wo, x2d, g1, b1, w_in, wd, layer, g2, b2, tm, dec=None):
    m, d = x2d.shape
    half = d // 2
    dff = wd.shape[1]
    nsteps = m // tm
    once = lambda shape, imap: pl.BlockSpec(shape, imap, pipeline_mode=pl.Buffered(1))
    in_specs = [pl.BlockSpec((tm, half), lambda i, *_: (i, h1_col)),
                pl.BlockSpec((tm, half), lambda i, *_: (i, h2_col)),
                once((d, d), lambda i, *_: (0, 0)),
                pl.BlockSpec((tm, d), lambda i, *_: (i, 0)),
                pl.BlockSpec((1, d), lambda i, *_: (0, 0)),
                pl.BlockSpec((1, d), lambda i, *_: (0, 0)),
                once((None, d, dff), lambda i, *_: (layer, 0, 0)),
                once((None, d, dff), lambda i, *_: (layer, 0, 1)),
                once((None, dff, d), lambda i, *_: (layer, 0, 0)),
                pl.BlockSpec((1, d), lambda i, *_: (0, 0)),
                pl.BlockSpec((1, d), lambda i, *_: (0, 0))]
    out_shape = [jax.ShapeDtypeStruct((m, d), F32)]
    out_specs = [pl.BlockSpec((tm, d), lambda i, *_: (i, 0))]
    args = [h1, h2, wo, x2d, g1, b1, w_in, w_in, wd, g2, b2]
    if dec is None:
        return pl.pallas_call(
            functools.partial(_tail_kernel, 0, 0), out_shape=out_shape, grid=(nsteps,), in_specs=in_specs,
            out_specs=out_specs, compiler_params=_cparams("parallel"), name="tail")(*args)[0]
    page_table, small, (ck, cv, clf), seq0, n_seq = dec
    n_pages = page_table.shape[1]
    g = PAGES_PER_STEP
    sps = n_pages // g
    assert nsteps == n_seq * sps
    seq_of = lambda i: seq0 + i // sps
    dec_specs, scratch = _decode_specs(small, seq_of, ck.shape[3], g)
    d_shapes, d_specs = _decode_outs(n_seq, lambda i: i // sps)
    grid_spec = pltpu.PrefetchScalarGridSpec(
        num_scalar_prefetch=1, grid=(nsteps,), in_specs=in_specs + dec_specs, out_specs=out_specs + d_specs,
        scratch_shapes=scratch)
    return pl.pallas_call(
        functools.partial(_tail_kernel, seq0, sps), out_shape=out_shape + d_shapes, grid_spec=grid_spec,
        compiler_params=_cparams("arbitrary"), name="tail_decode",
    )(page_table, *args, *small, ck, cv, clf)


def _gla_level_masks():
    c = GLA_CHUNK
    t = np.arange(c)[:, None]
    s = np.arange(c)[None, :]
    x = t ^ s
    masks = [x == 0] + [(t > s) & (x >= (1 << (l - 1))) & (x < (1 << l)) for l in range(1, GLA_LEVELS + 1)]
    m = np.stack(masks).astype(np.float32)
    return jnp.asarray(np.concatenate([m, m], axis=2))


def _gla_sum_matrices():
    c = GLA_CHUNK
    t = np.arange(c)[:, None]
    u = np.arange(c)[None, :]
    mq = [(u <= t)]
    mk = [(u > t)]
    for l in range(1, GLA_LEVELS + 1):
        blk, half = 1 << l, 1 << (l - 1)
        same = (t // blk) == (u // blk)
        mq.append(same & (t % blk >= half) & (u % blk >= half) & (u <= t))
        mk.append(same & (t % blk < half) & (u % blk < half) & (u > t))
    mk.append(np.ones((c, c), bool))
    mq = np.concatenate(mq, axis=0).astype(np.float32)
    mk = np.concatenate(mk, axis=0).astype(np.float32).T
    mq2 = np.concatenate([mq, mq], axis=1)
    mk2 = np.concatenate([mk, mk], axis=0)
    return jnp.asarray(mq2, BF16), jnp.asarray(mk2, BF16)


def _split_hi_lo(x):
    hi = x.astype(BF16)
    lo = (x - hi.astype(F32)).astype(BF16)
    return hi, lo


def _mix0_kernel(bg_ref, cg_ref, hv_ref, v_ref, gt_ref, q_ref, r_ref, kt_ref, rt_ref,
                 convw_ref, w2_ref, w2t_ref, bg2_ref, bg2c_ref, ng_ref, mq_ref, mk_ref, lvl_ref,
                 mixed_ref, convst_ref, glast_ref, s_ref, ext_ref):
    c = GLA_CHUNK
    rows = bg_ref.shape[0]
    i = pl.program_id(1)

    @pl.when(i == 0)
    def _():
        s_ref[...] = jnp.zeros_like(s_ref)
        ext_ref[0:8, :] = jnp.zeros((8, MIX_HALF), F32)

    u = cg_ref[...] * hv_ref[...]
    ext_ref[8:8 + rows, :] = u
    w = convw_ref[...]
    y = w[2:3, :] * u + w[1:2, :] * ext_ref[7:7 + rows, :] + w[0:1, :] * ext_ref[6:6 + rows, :]
    mixed_ref[:, 0:MIX_HALF] = (bg_ref[...] * y).astype(BF16)
    ext_ref[0:8, :] = u[rows - 8:rows, :]
    convst_ref[0] = u[rows - (SC_K - 1):rows, :]

    pair = 2 * GLA_DK
    own_block = ((lax.broadcasted_iota(jnp.int32, (pair, 2 * c), 0) < GLA_DK)
                 == (lax.broadcasted_iota(jnp.int32, (pair, 2 * c), 1) < c))

    def pair_dot(qp, ktp):
        kbd = jnp.where(own_block, jnp.concatenate([ktp, ktp], axis=1), 0.0).astype(BF16)
        return _dot(qp.astype(BF16), kbd)

    chunks = []
    for n in range(rows // c):
        rs = slice(n * c, (n + 1) * c)
        z = _dot(r_ref[rs, :].astype(BF16), w2_ref[...]) + bg2_ref[...]
        la = _log_sigmoid(z) * GLA_INV_TAU
        zt = _dot(w2t_ref[...], rt_ref[0, :, rs].astype(BF16)) + bg2c_ref[...]
        lat = _log_sigmoid(zt) * GLA_INV_TAU
        hi, lo = _split_hi_lo(la)
        eq = _dot(mq_ref[...], jnp.concatenate([hi, lo], axis=0))
        hit, lot = _split_hi_lo(lat)
        ek = _dot(jnp.concatenate([hit, lot], axis=1), mk_ref[...])
        chunks.append((rs, q_ref[rs, :] * (GLA_DK ** -0.5), kt_ref[0, :, rs], eq, ek))

    all_scores = []
    for rs, q, kt, eq, ek in chunks:
        scores = []
        for hp in range(GLA_HEADS // 2):
            ps = slice(hp * pair, (hp + 1) * pair)
            qp = q[:, ps]
            ktp = kt[ps, :]
            a = lvl_ref[0] * pair_dot(qp, ktp)
            for l in range(1, GLA_LEVELS + 1):
                ql = qp * jnp.exp(eq[l * c:(l + 1) * c, ps])
                kl = ktp * jnp.exp(ek[ps, l * c:(l + 1) * c])
                a = a + lvl_ref[l] * pair_dot(ql, kl)
            a = a.astype(BF16)
            scores += [a[:, 0:c], a[:, c:2 * c]]
        all_scores.append(scores)

    for (rs, q, kt, eq, ek), scores in zip(chunks, all_scores):
        for h in range(GLA_HEADS):
            ks = slice(h * GLA_DK, (h + 1) * GLA_DK)
            vs = slice(h * GLA_DV, (h + 1) * GLA_DV)
            qh = q[:, ks]
            kth = kt[ks, :]
            vh = v_ref[rs, vs].astype(BF16)
            s_old = s_ref[h]
            q_read = (qh * jnp.exp(eq[0:c, ks])).astype(BF16)
            o = _dot(scores[h], vh) + _dot(q_read, s_old.astype(BF16))
            k_write = (kth * jnp.exp(ek[ks, 0:c])).astype(BF16)
            a_chunk = jnp.exp(ek[ks, (GLA_LEVELS + 1) * c:(GLA_LEVELS + 2) * c])
            s_ref[h] = a_chunk * s_old + _dot(k_write, vh)
            on = o * lax.rsqrt(jnp.mean(o * o, axis=-1, keepdims=True) + EPS) * ng_ref[...]
            mixed_ref[rs, MIX_HALF + h * GLA_DV:MIX_HALF + (h + 1) * GLA_DV] = (
                on * _silu(gt_ref[rs, vs])).astype(BF16)
    glast_ref[0] = s_ref[...]


def _mix0(pn, kt, rt, conv_w, w2, w2t, bg2, bg2c, ng, nb):
    m = pn.shape[0]
    l = m // nb
    c = min(l, MIX_ROWS)
    assert c % GLA_CHUNK == 0 and l % c == 0
    nl = l // c
    mq, mk = _gla_sum_matrices()
    lvl = _gla_level_masks()
    row = lambda b, i: b * nl + i
    full = lambda a: pl.BlockSpec(a.shape, lambda b, i: (0,) * a.ndim)
    wide = lambda col: pl.BlockSpec((c, MIX_HALF), lambda b, i: (row(b, i), col))
    q_col = 5 * MIX_HALF // (GLA_HEADS * GLA_DK)
    r_col = (5 * MIX_HALF + GLA_HEADS * GLA_DK) // R_PAD
    return pl.pallas_call(
        _mix0_kernel,
        out_shape=[jax.ShapeDtypeStruct((m, D_MODEL), BF16),
                   jax.ShapeDtypeStruct((nb, SC_K - 1, MIX_HALF), F32),
                   jax.ShapeDtypeStruct((nb, GLA_HEADS, GLA_DK, GLA_DV), F32)],
        grid=(nb, nl),
        in_specs=[wide(0), wide(1), wide(2), wide(3), wide(4),
                  pl.BlockSpec((c, GLA_HEADS * GLA_DK), lambda b, i: (row(b, i), q_col)),
                  pl.BlockSpec((c, R_PAD), lambda b, i: (row(b, i), r_col)),
                  pl.BlockSpec((1, GLA_HEADS * GLA_DK, c), lambda b, i: (b, 0, i)),
                  pl.BlockSpec((1, GLA_RANK, c), lambda b, i: (b, 0, i)),
                  full(conv_w), full(w2), full(w2t), full(bg2), full(bg2c), full(ng), full(mq), full(mk),
                  full(lvl)],
        out_specs=[pl.BlockSpec((c, D_MODEL), lambda b, i: (row(b, i), 0)),
                   pl.BlockSpec((1, SC_K - 1, MIX_HALF), lambda b, i: (b, 0, 0)),
                   pl.BlockSpec((1, GLA_HEADS, GLA_DK, GLA_DV), lambda b, i: (b, 0, 0, 0))],
        scratch_shapes=[pltpu.VMEM((GLA_HEADS, GLA_DK, GLA_DV), F32), pltpu.VMEM((c + 8, MIX_HALF), F32)],
        compiler_params=_cparams("parallel", "arbitrary"),
        name="mix0",
    )(pn, pn, pn, pn, pn, pn, pn, kt, rt, conv_w, w2, w2t, bg2, bg2c, ng, mq, mk, lvl)


C_ROWS = 16
LOG2E = 1.4426950408889634


def _fox_c_kernel(ft_ref, bf_ref, lf_ref, c_ref):
    lf = _log_sigmoid(ft_ref[0] + bf_ref[...])
    lf_ref[0] = lf
    n = lf.shape[1]
    lane = lax.broadcasted_iota(jnp.int32, lf.shape, 1)
    suf = lf
    s = 1
    while s < n:
        suf = suf + jnp.where(lane + s < n, pltpu.roll(suf, n - s, 1), 0.0)
        s *= 2
    bias = (suf - lf) * LOG2E
    hi = bias.astype(BF16).astype(F32)
    r1 = bias - hi
    mid = r1.astype(BF16).astype(F32)
    lo = (r1 - mid).astype(BF16).astype(F32)
    row = lax.broadcasted_iota(jnp.int32, (C_ROWS, n), 0)
    for h in range(FOX_HEADS):
        hs = slice(h, h + 1)
        parts = jnp.where(row == 0, hi[hs], jnp.where(row == 1, mid[hs], jnp.where(row == 2, lo[hs], 0.0)))
        c_ref[0, h] = parts.astype(BF16)


def _fox_c(ft, bf_col):
    nb, _, l = ft.shape
    spec = pl.BlockSpec((1, FOX_HEADS, l), lambda b: (b, 0, 0))
    return pl.pallas_call(
        _fox_c_kernel,
        out_shape=[jax.ShapeDtypeStruct((nb, FOX_HEADS, l), F32),
                   jax.ShapeDtypeStruct((nb, FOX_HEADS, C_ROWS, l), BF16)],
        grid=(nb,),
        in_specs=[spec, pl.BlockSpec((FOX_HEADS, 1), lambda b: (0, 0))],
        out_specs=[spec, pl.BlockSpec((1, FOX_HEADS, C_ROWS, l), lambda b: (b, 0, 0, 0))],
        compiler_params=_cparams("parallel"),
        name="fox_c",
    )(ft, bf_col)


AUG = 2 * FOX_HD
FLASH_AHEAD = 8


def _fox_flash_kernel(dec_pages, steps_per_seq, seq0, n_seq, *refs):
    sched_ref = refs[0]
    if dec_pages:
        pt_ref = refs[1]
        refs = refs[2:]
    else:
        refs = refs[1:]
    q_ref, kt_ref, vt_ref, c_ref = refs[:4]
    refs = refs[4:]
    t = pl.program_id(0)
    if dec_pages:
        dec_in, refs = refs[:N_DEC_IN], refs[N_DEC_IN:]
        o_ref, do_ref, dlf_ref, qa_ref, m_ref, acc_ref = refs[:6]
        seq_of = lambda s: seq0 + jnp.minimum(s // steps_per_seq, n_seq - 1)
        dec, drain = _decode_begin(pt_ref, dec_in, (do_ref, dlf_ref), refs[6:], t, pl.num_programs(0),
                                   seq_of, steps_per_seq, dec_pages)
        dacc_ref = dec[13]
    else:
        o_ref, qa_ref, m_ref, acc_ref = refs
    qi = sched_ref[1, t]
    kj = sched_ref[2, t]
    tq = q_ref.shape[0]
    tk = kt_ref.shape[2]
    lane = lax.broadcasted_iota(jnp.int32, (tq, AUG), 1)
    if dec_pages:
        dstep = t % steps_per_seq
        _decode_init(dstep, *dec)

    @pl.when(kj == 0)
    def _():
        ones3 = jnp.where(lane < FOX_HD + 3, 1.0, 0.0)
        for p in range(FOX_HEADS // 2):
            qp = q_ref[:, p * AUG:(p + 1) * AUG] * (FOX_HD ** -0.5 * LOG2E)
            qa_ref[2 * p] = jnp.where(lane < FOX_HD, qp, ones3).astype(BF16)
            qa_ref[2 * p + 1] = jnp.where(lane < FOX_HD, pltpu.roll(qp, FOX_HD, 1), ones3).astype(BF16)
        m_ref[...] = jnp.full_like(m_ref, -jnp.inf)
        acc_ref[...] = jnp.zeros_like(acc_ref)

    def block(masked):
        if masked:
            visible = (lax.broadcasted_iota(jnp.int32, (tq, tk), 1) <= lax.broadcasted_iota(jnp.int32, (tq, tk), 0))
        pad_k = jnp.zeros((AUG - FOX_HD - C_ROWS, tk), BF16)
        ones_row = jnp.where(lax.broadcasted_iota(jnp.int32, (AUG - FOX_HD, tk), 0) == 0, 1.0, 0.0).astype(BF16)

        def logits(h):
            hs = slice(h * FOX_HD, (h + 1) * FOX_HD)
            ka = jnp.concatenate([kt_ref[0, hs, :], c_ref[0, h], pad_k], axis=0)
            return _dot(qa_ref[h], ka)

        s_ahead = [logits(h) for h in range(FLASH_AHEAD)]
        if dec_pages:
            alpha_d, p_d = _decode_logits(*dec)
        for h in range(FOX_HEADS):
            if dec_pages and h == FOX_HEADS // 2:
                _decode_values(alpha_d, p_d, dec[6], dacc_ref)
            if h + FLASH_AHEAD < FOX_HEADS:
                s_ahead.append(logits(h + FLASH_AHEAD))
            s = s_ahead.pop(0)
            hs = slice(h * FOX_HD, (h + 1) * FOX_HD)
            va = jnp.concatenate([vt_ref[0, hs, :], ones_row], axis=0)
            if masked:
                s = jnp.where(visible, s, -jnp.inf)
            m_old = m_ref[h]
            m_new = jnp.maximum(m_old, jnp.max(s, axis=-1, keepdims=True))
            pr = jnp.exp2(s - jnp.concatenate([m_new] * (tk // AUG), axis=1)).astype(BF16)
            acc_ref[h] = jnp.exp2(m_old - m_new) * acc_ref[h] + _dot_nt(pr, va)
            m_ref[h] = m_new

    @pl.when(kj < qi)
    def _():
        block(False)

    @pl.when(kj == qi)
    def _():
        block(True)
        for p in range(FOX_HEADS // 2):
            a0 = acc_ref[2 * p]
            a1 = acc_ref[2 * p + 1]
            o0 = a0 * (1.0 / a0[:, FOX_HD:FOX_HD + 1])
            o1 = a1 * (1.0 / a1[:, FOX_HD:FOX_HD + 1])
            o_ref[:, p * AUG:(p + 1) * AUG] = jnp.where(lane < FOX_HD, o0, pltpu.roll(o1, FOX_HD, 1)).astype(BF16)

    if dec_pages:
        _decode_finish(dstep, steps_per_seq, do_ref, dec[11], dacc_ref)
        drain()


def _fox_flash(pn, kt, vt, caug, nb, tq, dec=None):
    m = pn.shape[0]
    l = m // nb
    nq = l // tq
    sched = jnp.asarray(np.array([(b, i, j) for b in range(nb) for i in range(nq) for j in range(i + 1)],
                                 np.int32).T)
    steps = sched.shape[1]
    kv_spec = pl.BlockSpec((1, MIX_HALF, tq), lambda t, sc, *_: (sc[0, t], 0, sc[2, t]))
    q_map = lambda t, sc, *_: (sc[0, t] * nq + sc[1, t], 0)
    in_specs = [pl.BlockSpec((tq, MIX_HALF), q_map), kv_spec, kv_spec,
                pl.BlockSpec((1, FOX_HEADS, C_ROWS, tq), lambda t, sc, *_: (sc[0, t], 0, 0, sc[2, t]))]
    out_shape = [jax.ShapeDtypeStruct((m, MIX_HALF), BF16)]
    out_specs = [pl.BlockSpec((tq, MIX_HALF), q_map)]
    scratch = [pltpu.VMEM((FOX_HEADS, tq, AUG), BF16), pltpu.VMEM((FOX_HEADS, tq, AUG), F32),
               pltpu.VMEM((FOX_HEADS, tq, AUG), F32)]
    if dec is None:
        grid_spec = pltpu.PrefetchScalarGridSpec(
            num_scalar_prefetch=1, grid=(steps,), in_specs=in_specs, out_specs=out_specs, scratch_shapes=scratch)
        return pl.pallas_call(
            functools.partial(_fox_flash_kernel, 0, 0, 0, 0), out_shape=out_shape, grid_spec=grid_spec,
            compiler_params=_cparams("arbitrary"), name="fox_flash")(sched, pn, kt, vt, caug)[0]
    page_table, small, (ck, cv, clf), seq0, n_seq, g = dec
    n_pages = page_table.shape[1]
    sps = n_pages // g
    slots = steps // sps
    assert steps % sps == 0 and slots >= n_seq
    seq_of = lambda t, *_: seq0 + jnp.minimum(t // sps, n_seq - 1)
    dec_specs, dec_scratch = _decode_specs(small, seq_of, ck.shape[3], g)
    d_shapes, d_specs = _decode_outs(slots, lambda t, *_: t // sps)
    grid_spec = pltpu.PrefetchScalarGridSpec(
        num_scalar_prefetch=2, grid=(steps,), in_specs=in_specs + dec_specs, out_specs=out_specs + d_specs,
        scratch_shapes=scratch + dec_scratch)
    o_c, o_part, lf_part = pl.pallas_call(
        functools.partial(_fox_flash_kernel, g, sps, seq0, n_seq), out_shape=out_shape + d_shapes,
        grid_spec=grid_spec, compiler_params=_cparams("arbitrary"), name="fox_flash_decode",
    )(sched, page_table, pn, kt, vt, caug, *small, ck, cv, clf)
    return o_c, o_part[:n_seq], lf_part[:n_seq]


CF_HALO = 32
SUBLANES = 8


CONF_ROWS = 256


def _conf_rows(a, gate, w_ref, cb_ref, g_ref, b_ref, ext_ref):
    t = a.shape[0]
    u = a * _sigmoid(gate)
    ext_ref[CF_HALO:CF_HALO + t, :] = u
    off = CF_HALO - (CF_K - 1)
    acc = jnp.zeros((t, MIX_HALF), F32) + cb_ref[...]
    for rho in range(SUBLANES):
        taps = [j for j in range(CF_K) if (off + j) % SUBLANES == rho]
        rows = t if rho == 0 else t + SUBLANES
        part = jnp.zeros((rows, MIX_HALF), F32)
        for j in taps:
            a0 = off + j - rho
            part = part + w_ref[j:j + 1, :] * ext_ref[a0:a0 + rows, :]
        acc = acc + part[rho:rho + t, :]
    ext_ref[0:CF_HALO, :] = ext_ref[t:t + CF_HALO, :]
    wide = lambda r: jnp.concatenate([r] * (MIX_HALF // LANES), axis=1)
    d = acc - wide(_lane_sum_mxu_wide(acc) * (1.0 / MIX_HALF))
    inv = lax.rsqrt(_lane_sum_mxu_wide(d * d) * (1.0 / MIX_HALF) + EPS)
    return _silu(d * wide(inv) * g_ref[...] + b_ref[...]).astype(BF16)


def _proj_conf_kernel(t_outs, x_ref, wn_ref, wt_ref, w_ref, cb_ref, g_ref, b_ref, q_ref, y_ref, st_ref, *rest):
    pt_refs, ext_ref = rest[:-1], rest[-1]
    tl = x_ref.shape[0]
    sub = min(tl, CONF_ROWS)

    @pl.when(pl.program_id(1) == 0)
    def _():
        ext_ref[0:CF_HALO, :] = jnp.zeros((CF_HALO, MIX_HALF), F32)

    xb = x_ref[...].astype(BF16)
    n_groups = tl // sub
    group = lambda n: _dot(xb[n * sub:(n + 1) * sub], wn_ref[...])
    pn_next = group(0)
    pt = None
    for n in range(n_groups):
        pn = pn_next
        if n + 1 < n_groups:
            pn_next = group(n + 1)
        else:
            pt = _dot_nt(wt_ref[...], xb)
        rs = slice(n * sub, (n + 1) * sub)
        q_ref[rs, :] = pn[:, 0:MIX_HALF]
        y_ref[rs, :] = _conf_rows(pn[:, MIX_HALF:2 * MIX_HALF], pn[:, 2 * MIX_HALF:3 * MIX_HALF],
                                  w_ref, cb_ref, g_ref, b_ref, ext_ref)
    st_ref[0] = ext_ref[CF_HALO - (CF_K - 1):CF_HALO, :]
    for ref, (r0, n, dt) in zip(pt_refs, t_outs):
        ref[0] = pt[r0:r0 + n].astype(dt)


def _proj_conf(x2d, wn, wt, t_outs, w, cb, g, b, nb, tl):
    m, k = x2d.shape
    l = m // nb
    nl = l // tl
    full = lambda a: pl.BlockSpec(a.shape, lambda bb, i: (0,) * a.ndim)
    rows = lambda bb, i: (bb * nl + i, 0)
    out_shape = [jax.ShapeDtypeStruct((m, MIX_HALF), F32), jax.ShapeDtypeStruct((m, MIX_HALF), BF16),
                 jax.ShapeDtypeStruct((nb, CF_K - 1, MIX_HALF), F32)]
    out_shape += [jax.ShapeDtypeStruct((nb, n, l), dt) for _, n, dt in t_outs]
    out_specs = [pl.BlockSpec((tl, MIX_HALF), rows), pl.BlockSpec((tl, MIX_HALF), rows),
                 pl.BlockSpec((1, CF_K - 1, MIX_HALF), lambda bb, i: (bb, 0, 0))]
    out_specs += [pl.BlockSpec((1, n, tl), lambda bb, i: (bb, 0, i)) for _, n, _ in t_outs]
    return pl.pallas_call(
        functools.partial(_proj_conf_kernel, t_outs),
        out_shape=out_shape,
        grid=(nb, nl),
        in_specs=[pl.BlockSpec((tl, k), rows), full(wn), full(wt), full(w), full(cb), full(g), full(b)],
        out_specs=out_specs,
        scratch_shapes=[pltpu.VMEM((min(tl, CONF_ROWS) + CF_HALO, MIX_HALF), F32)],
        compiler_params=_cparams("parallel", "arbitrary"),
        name="proj_conf",
    )(x2d, wn, wt, w, cb, g, b)


def _smix0_kernel(pn_ref, kc_ref, qc_ref, rc_ref, prev_ref, s_ref, convw_ref, w2t_ref, bg2c_ref, ng_ref,
                  mixed_ref, convst_ref, sout_ref):
    w = convw_ref[...]
    w2t = w2t_ref[...]
    for i in range(pn_ref.shape[0]):
        pn = pn_ref[i]
        bg, cg, hv = pn[:, 0:MIX_HALF], pn[:, MIX_HALF:2 * MIX_HALF], pn[:, 2 * MIX_HALF:3 * MIX_HALF]
        v, gt = pn[:, 3 * MIX_HALF:4 * MIX_HALF], pn[:, 4 * MIX_HALF:5 * MIX_HALF]
        u = cg * hv
        prev = prev_ref[i]
        y = w[0:1, :] * prev[0:1, :] + w[1:2, :] * prev[1:2, :] + w[2:3, :] * u
        mixed_ref[i, :, 0:MIX_HALF] = (bg * y).astype(BF16)
        convst_ref[i, 0:1, :] = prev[1:2, :]
        convst_ref[i, 1:2, :] = u

        zc = _dot(w2t, rc_ref[i].astype(BF16)) + bg2c_ref[...]
        ac = jnp.exp(_log_sigmoid(zc) * GLA_INV_TAU)
        qc = qc_ref[i] * (GLA_DK ** -0.5)
        kc = kc_ref[i]
        for h in range(GLA_HEADS):
            ks = slice(h * GLA_DK, (h + 1) * GLA_DK)
            vs = slice(h * GLA_DV, (h + 1) * GLA_DV)
            s_new = ac[ks, :] * s_ref[i, h] + kc[ks, :] * v[:, vs]
            sout_ref[i, h] = s_new
            o = jnp.sum(qc[ks, :] * s_new, axis=0, keepdims=True)
            on = o * lax.rsqrt(jnp.mean(o * o, axis=-1, keepdims=True) + EPS) * ng_ref[...]
            mixed_ref[i, :, MIX_HALF + h * GLA_DV:MIX_HALF + (h + 1) * GLA_DV] = (
                on * _silu(gt[:, vs])).astype(BF16)


def _smix0(pn3, kc, qc, rc, prev, s0, conv_w, w2t, bg2c, ng):
    nd, _, nn = pn3.shape
    sb = _tile_rows(nd, SAMPLE_SEQS_PER_STEP)
    per = lambda a: pl.BlockSpec((sb,) + a.shape[1:], lambda b: (b,) + (0,) * (a.ndim - 1))
    full = lambda a: pl.BlockSpec(a.shape, lambda b: (0,) * a.ndim)
    out_shape = [jax.ShapeDtypeStruct((nd, 1, D_MODEL), BF16),
                 jax.ShapeDtypeStruct((nd, SC_K - 1, MIX_HALF), F32),
                 jax.ShapeDtypeStruct(s0.shape, F32)]
    return pl.pallas_call(
        _smix0_kernel,
        out_shape=out_shape,
        grid=(nd // sb,),
        in_specs=[per(pn3), per(kc), per(qc), per(rc), per(prev), per(s0),
                  full(conv_w), full(w2t), full(bg2c), full(ng)],
        out_specs=[per(o) for o in out_shape],
        compiler_params=_cparams("parallel"),
        name="smix0",
    )(pn3, kc, qc, rc, prev, s0, conv_w, w2t, bg2c, ng)


def _sconf_kernel(pn_ref, prev_ref, w_ref, cb_ref, g_ref, b_ref, y_ref, st_ref):
    w = w_ref[...]
    for i in range(pn_ref.shape[0]):
        pn = pn_ref[i]
        a, gate = pn[:, MIX_HALF:2 * MIX_HALF], pn[:, 2 * MIX_HALF:3 * MIX_HALF]
        u = a * _sigmoid(gate)
        prev = prev_ref[i]
        acc = jnp.sum(w[0:CF_K - 1, :] * prev, axis=0, keepdims=True) + w[CF_K - 1:CF_K, :] * u + cb_ref[...]
        y_ref[i] = _silu(_layernorm(acc, g_ref[...], b_ref[...])).astype(BF16)
        st_ref[i, 0:CF_K - 2, :] = prev_ref[i, 1:CF_K - 1, :]
        st_ref[i, CF_K - 2:CF_K - 1, :] = u


def _sconf(pn3, prev, w, cb, g, b):
    nd = pn3.shape[0]
    sb = _tile_rows(nd, SAMPLE_SEQS_PER_STEP)
    per = lambda a: pl.BlockSpec((sb,) + a.shape[1:], lambda bb: (bb,) + (0,) * (a.ndim - 1))
    full = lambda a: pl.BlockSpec(a.shape, lambda bb: (0,) * a.ndim)
    out_shape = [jax.ShapeDtypeStruct((nd, 1, MIX_HALF), BF16), jax.ShapeDtypeStruct(prev.shape, F32)]
    return pl.pallas_call(
        _sconf_kernel,
        out_shape=out_shape,
        grid=(nd // sb,),
        in_specs=[per(pn3), per(prev), full(w), full(cb), full(g), full(b)],
        out_specs=[per(o) for o in out_shape],
        compiler_params=_cparams("parallel"),
        name="sconf",
    )(pn3, prev, w, cb, g, b)


def _decode_kernel(seq0, steps_per_seq, pt_ref, *refs):
    u = pl.program_id(0)
    step = u % steps_per_seq
    dec, drain = _decode_begin(pt_ref, refs[:N_DEC_IN], refs[N_DEC_IN:N_DEC_IN + 2], refs[N_DEC_IN + 2:], u,
                               pl.num_programs(0), lambda s: seq0 + s // steps_per_seq, steps_per_seq,
                               PAGES_PER_STEP)
    _decode_init(step, *dec)
    alpha, p_pages = _decode_logits(*dec)
    _decode_values(alpha, p_pages, dec[6], dec[13])
    _decode_finish(step, steps_per_seq, dec[8], dec[11], dec[13])
    drain()


def _decode(page_table, small, caches, seq0, n_seq):
    ck, cv, clf = caches
    g = PAGES_PER_STEP
    sps = page_table.shape[1] // g
    in_specs, scratch = _decode_specs(small, lambda u: seq0 + u // sps, ck.shape[3], g)
    out_shape, out_specs = _decode_outs(n_seq, lambda u: u // sps)
    grid_spec = pltpu.PrefetchScalarGridSpec(
        num_scalar_prefetch=1, grid=(n_seq * sps,), in_specs=in_specs, out_specs=out_specs,
        scratch_shapes=scratch)
    return pl.pallas_call(
        functools.partial(_decode_kernel, seq0, sps),
        out_shape=out_shape,
        grid_spec=grid_spec,
        compiler_params=_cparams("arbitrary"),
        name="fox_decode",
    )(page_table, *small, ck, cv, clf)


def _tile_rows(n, pref):
    t = min(n, pref)
    while n % t:
        t //= 2
    return t


def kernel(x_prompt, x_sample, state_conv_a, state_gla, cache_k, cache_v, cache_logf, state_conv_d, page_table,
           ab_w_in, ab_conv_w, gla_w_gate2, gla_b_gate, gla_norm_g, ab_w_out,
           cd_w_in, fox_b_f, cf_conv_w, cf_conv_b, cf_ln_g, cf_ln_b, cd_w_out,
           ffn_w_in, ffn_w_out, ln_g, ln_b):
    nb, seq, d = x_prompt.shape
    nd = x_sample.shape[0]
    hk = GLA_HEADS * GLA_DK

    o_bg, o_cg, o_hv = 0, MIX_HALF, 2 * MIX_HALF
    o_q = 3 * MIX_HALF
    o_k = o_q + hk
    o_v = o_k + hk
    o_g = o_v + MIX_HALF
    o_r = o_g + MIX_HALF
    wab = ab_w_in.astype(BF16)
    ab_wn = jnp.concatenate([wab[:, o_bg:o_q], wab[:, o_v:o_g], wab[:, o_g:o_r], wab[:, o_q:o_k],
                             jnp.pad(wab[:, o_r:o_r + GLA_RANK], ((0, 0), (0, R_PAD - GLA_RANK)))], axis=1)
    wab_t = wab.T
    ab_wt_p = jnp.concatenate([wab_t[o_k:o_v], wab_t[o_r:o_r + GLA_RANK]], axis=0)
    ab_wt_s = jnp.concatenate([ab_wt_p, wab_t[o_q:o_k]], axis=0)
    w2 = jnp.pad(gla_w_gate2.astype(BF16), ((0, R_PAD - GLA_RANK), (0, 0)))
    w2t = gla_w_gate2.T.astype(BF16)
    bg2 = gla_b_gate.reshape(1, hk)
    bg2c = gla_b_gate.reshape(hk, 1)
    ng = gla_norm_g.reshape(1, GLA_DV)
    ab_wo = ab_w_out.astype(BF16)

    wcd = cd_w_in.astype(BF16)
    c_q, c_k, c_v = 0, MIX_HALF, 2 * MIX_HALF
    c_f = 3 * MIX_HALF
    c_a = c_f + FOX_HEADS
    c_gate = c_a + MIX_HALF
    cd_wn = jnp.concatenate([wcd[:, c_q:c_k], wcd[:, c_a:c_gate], wcd[:, c_gate:c_gate + MIX_HALF]], axis=1)
    cd_wn_s = jnp.concatenate([cd_wn, wcd[:, c_k:c_f]], axis=1)
    wcd_t = wcd.T
    cd_wt_p = jnp.concatenate([wcd_t[c_k:c_v], wcd_t[c_v:c_f],
                               jnp.pad(wcd_t[c_f:c_a], ((0, F_PAD - FOX_HEADS), (0, 0)))], axis=0)
    cd_wt_s = jnp.concatenate([cd_wt_p, wcd_t[c_q:c_k]], axis=0)
    bf_col = fox_b_f.reshape(FOX_HEADS, 1)
    cb = cf_conv_b.reshape(1, MIX_HALF)
    cg_ = cf_ln_g.reshape(1, MIX_HALF)
    cbb = cf_ln_b.reshape(1, MIX_HALF)
    cd_wo = cd_w_out.astype(BF16)
    ffn_wi = ffn_w_in.astype(BF16)
    ffn_wo = ffn_w_out.astype(BF16)

    def tail(x2d, h1, c1, h2, c2, wo, layer, tm, dec=None):
        return _tail(h1, c1, h2, c2, wo, x2d, ln_g[layer, 0:1], ln_b[layer, 0:1], ffn_wi, ffn_wo, layer,
                     ln_g[layer, 1:2], ln_b[layer, 1:2], tm, dec)

    xs = x_sample.reshape(nd, d)
    pn_s, kc_s, rc_s, qc_s = _proj(xs, ab_wn, ab_wt_s,
                                   ((0, hk, F32), (hk, GLA_RANK, F32), (hk + GLA_RANK, hk, F32)), 1, nd,
                                   as_columns=True)
    mixed_s, conv_a_s, gla_s = _smix0(pn_s.reshape(nd, 1, -1), kc_s, qc_s, rc_s,
                                      state_conv_a, state_gla, ab_conv_w, w2t, bg2c, ng)
    mixed_s = mixed_s.reshape(nd, d)
    xs2 = tail(xs, mixed_s, 0, mixed_s, 1, ab_wo, 0, nd)
    kvf = ((0, MIX_HALF, F32), (MIX_HALF, MIX_HALF, F32), (2 * MIX_HALF, F_PAD, F32))
    cols_s = ((0, MIX_HALF, F32), (MIX_HALF, MIX_HALF, F32), (2 * MIX_HALF, FOX_HEADS, F32),
              (2 * MIX_HALF + F_PAD, MIX_HALF, F32))
    pn1_s, kc1_s, vc1_s, fc1_s, qc1_s = _proj(xs2, cd_wn_s, cd_wt_s, cols_s, 1, nd, as_columns=True)
    caches = (jnp.transpose(cache_k, (0, 2, 3, 1)), jnp.transpose(cache_v, (0, 2, 3, 1)),
              jnp.transpose(cache_logf, (0, 2, 1)))
    dec_small = (qc1_s, kc1_s, vc1_s, fc1_s, bf_col)

    m = nb * seq
    tm = _tile_rows(m, TAIL_ROWS)
    tl = _tile_rows(seq, SEQ_TILE)
    n_pages = page_table.shape[1]
    per_tail = 0
    if n_pages % PAGES_PER_STEP == 0 and (m // tm) % (n_pages // PAGES_PER_STEP) == 0:
        per_tail = (m // tm) // (n_pages // PAGES_PER_STEP)
        if DEPTH * per_tail > nd:
            per_tail = 0
    in_tails = DEPTH * per_tail
    flash_steps = nb * (seq // tl) * (seq // tl + 1) // 2
    in_flash = 0
    if n_pages % FLASH_PAGES == 0 and flash_steps % (n_pages // FLASH_PAGES) == 0:
        in_flash = max(0, min(flash_steps // (n_pages // FLASH_PAGES), nd - in_tails))
    dec_parts = {}

    def prompt_tail(x2d, h1, c1, h2, c2, wo, layer):
        if per_tail == 0:
            return tail(x2d, h1, c1, h2, c2, wo, layer, tm)
        y, o_part, lf_part = tail(x2d, h1, c1, h2, c2, wo, layer, tm,
                                  (page_table, dec_small, caches, layer * per_tail, per_tail))
        dec_parts[layer * per_tail] = (o_part, lf_part)
        return y

    x0 = x_prompt.reshape(m, d)
    pn, kt, rt = _proj(x0, ab_wn, ab_wt_p, ((0, hk, F32), (hk, GLA_RANK, F32)), nb, tl)
    mixed, conv_a_p, gla_p = _mix0(pn, kt, rt, ab_conv_w, w2, w2t, bg2, bg2c, ng, nb)
    x2 = prompt_tail(x0, mixed, 0, mixed, 1, ab_wo, 0)

    pn1, y_d, conv_d_p, kt1, vt1, ft1, kt16, vt16 = _proj_conf(
        x2, cd_wn, cd_wt_p, kvf + ((0, MIX_HALF, BF16), (MIX_HALF, MIX_HALF, BF16)),
        cf_conv_w, cb, cg_, cbb, nb, tl)
    lft, caug = _fox_c(ft1, bf_col)
    if in_flash:
        o_c, o_part, lf_part = _fox_flash(pn1, kt16, vt16, caug, nb, tl,
                                          (page_table, dec_small, caches, in_tails, in_flash, FLASH_PAGES))
        dec_parts[in_tails] = (o_part, lf_part)
    else:
        o_c = _fox_flash(pn1, kt16, vt16, caug, nb, tl)
    y_p = prompt_tail(x2, o_c, 0, y_d, 0, cd_wo, 1)

    y_prompt = y_p.reshape(nb, seq, d)
    k_p = jnp.transpose(kt1.reshape(nb, FOX_HEADS, FOX_HD, seq), (0, 3, 1, 2))
    v_p = jnp.transpose(vt1.reshape(nb, FOX_HEADS, FOX_HD, seq), (0, 3, 1, 2))
    lf_p = jnp.transpose(lft, (0, 2, 1))

    done = in_tails + in_flash
    if done < nd:
        dec_parts[done] = tuple(_decode(page_table, dec_small, caches, done, nd - done))
    o_col = jnp.concatenate([dec_parts[k][0] for k in sorted(dec_parts)], axis=0)
    lf_col = jnp.concatenate([dec_parts[k][1] for k in sorted(dec_parts)], axis=0)
    o_cs = o_col.reshape(nd, MIX_HALF).astype(BF16)
    y_ds, conv_d_s = _sconf(pn1_s.reshape(nd, 1, -1), state_conv_d, cf_conv_w, cb, cg_, cbb)
    y_s = tail(xs2, o_cs, 0, y_ds.reshape(nd, MIX_HALF), 0, cd_wo, 1, nd)

    y_sample = y_s.reshape(nd, 1, d)
    k_s = pn1_s[:, 3 * MIX_HALF:4 * MIX_HALF].reshape(nd, 1, FOX_HEADS, FOX_HD)
    v_s = pn1_s[:, 4 * MIX_HALF:5 * MIX_HALF].reshape(nd, 1, FOX_HEADS, FOX_HD)
    lf_s = lf_col.reshape(nd, 1, FOX_HEADS)

    return (y_prompt, y_sample, conv_a_p, conv_a_s, gla_p, gla_s, k_p, k_s, v_p, v_s, lf_p, lf_s,
            conv_d_p, conv_d_s)
```

```python
import functools

import numpy as np
import jax
import jax.numpy as jnp
from jax import lax
from jax.experimental import pallas as pl
from jax.experimental.pallas import tpu as pltpu

F32 = jnp.float32
BF16 = jnp.bfloat16

D_MODEL = 1024
MIX_HALF = D_MODEL // 2
SC_K = 3
GLA_HEADS = 4
GLA_DV = MIX_HALF // GLA_HEADS
GLA_DK = GLA_DV // 2
GLA_RANK = 16
GLA_INV_TAU = 1.0 / 16.0
GLA_CHUNK = 128
GLA_LEVELS = 7
MIX_ROWS = 512
FOX_HEADS = 8
FOX_HD = MIX_HALF // FOX_HEADS
CF_K = 31
D_FF = 2816
DEPTH = 2
DN_ALPHA = (2 * DEPTH) ** 0.25
EPS = 1e-5
R_PAD = 128
F_PAD = 16
VMEM_LIMIT = 56 * 1024 * 1024
PAGES_PER_STEP = 16
FLASH_PAGES = 16
TAIL_ROWS = 256
SEQ_TILE = 512
SAMPLE_SEQS_PER_STEP = 8


def _cparams(*sem):
    return pltpu.CompilerParams(dimension_semantics=sem, vmem_limit_bytes=VMEM_LIMIT)


def _log_sigmoid(x):
    return jnp.minimum(x, 0.0) - jnp.log1p(jnp.exp(-jnp.abs(x)))


def _sigmoid(x):
    return 1.0 / (1.0 + jnp.exp(-x))


def _silu(x):
    return x * _sigmoid(x)


def _layernorm(y, g, b):
    mu = jnp.mean(y, axis=-1, keepdims=True)
    d = y - mu
    var = jnp.mean(d * d, axis=-1, keepdims=True)
    return d * lax.rsqrt(var + EPS) * g + b


def _dot(a, b):
    return jnp.dot(a, b, preferred_element_type=F32)


def _dot_nt(a, b):
    return lax.dot_general(a, b, (((1,), (1,)), ((), ())), preferred_element_type=F32)


def _proj_kernel(t_outs, as_columns, x_ref, wn_ref, wt_ref, pn_ref, *pt_refs):
    xb = x_ref[...].astype(BF16)
    pn_ref[...] = _dot(xb, wn_ref[...])
    pt = _dot_nt(wt_ref[...], xb)
    for ref, (r0, n, dt) in zip(pt_refs, t_outs):
        if as_columns:
            for row in range(x_ref.shape[0]):
                ref[row] = pt[r0:r0 + n, row:row + 1].astype(dt)
        else:
            ref[0] = pt[r0:r0 + n].astype(dt)


def _proj(x2d, wn, wt, t_outs, nb, tl, as_columns=False):
    m, k = x2d.shape
    l = m // nb
    nl = l // tl
    nn = wn.shape[1]
    nt = wt.shape[0]
    out_shape = [jax.ShapeDtypeStruct((m, nn), F32)]
    out_specs = [pl.BlockSpec((tl, nn), lambda b, i: (b * nl + i, 0))]
    if as_columns:
        assert nb == 1 and nl == 1
        out_shape += [jax.ShapeDtypeStruct((m, n, 1), dt) for _, n, dt in t_outs]
        out_specs += [pl.BlockSpec((m, n, 1), lambda b, i: (0, 0, 0)) for _, n, _ in t_outs]
    else:
        out_shape += [jax.ShapeDtypeStruct((nb, n, l), dt) for _, n, dt in t_outs]
        out_specs += [pl.BlockSpec((1, n, tl), lambda b, i: (b, 0, i)) for _, n, _ in t_outs]
    return pl.pallas_call(
        functools.partial(_proj_kernel, t_outs, as_columns),
        out_shape=out_shape,
        grid=(nb, nl),
        in_specs=[pl.BlockSpec((tl, k), lambda b, i: (b * nl + i, 0)),
                  pl.BlockSpec((k, nn), lambda b, i: (0, 0)),
                  pl.BlockSpec((nt, k), lambda b, i: (0, 0))],
        out_specs=out_specs,
        compiler_params=_cparams("parallel", "arbitrary"),
        name="proj",
    )(x2d, wn, wt)


LANES = 128


def _lane_sum_mxu_wide(x):
    ones = jnp.ones((x.shape[1], LANES), BF16)
    hi, lo = _split_hi_lo(x)
    return _dot(hi, ones) + _dot(lo, ones)


def _lane_sum_mxu(x):
    return _lane_sum_mxu_wide(x)[:, 0:1]


def _per_head_rows(a):
    return jnp.concatenate([jnp.broadcast_to(a[h:h + 1, :], (FOX_HD, 1)) for h in range(FOX_HEADS)], axis=0)


def _decode_init(step, qc_ref, kc_ref, vc_ref, fc_ref, bf_ref, k_refs, v_refs, lf_refs,
                 o_ref, lfo_ref, m_ref, l_ref, r_ref, acc_ref):
    @pl.when(step == 0)
    def _():
        qc = qc_ref[0] * (FOX_HD ** -0.5)
        lf_new = _log_sigmoid(fc_ref[0] + bf_ref[...])
        lfo_ref[0] = lf_new
        r_ref[...] = lf_new
        l_ref[...] = jnp.ones_like(l_ref)
        head_col = lax.broadcasted_iota(jnp.int32, (FOX_HEADS, 1), 0)
        s_self = jnp.zeros((FOX_HEADS, 1), F32)
        for h in range(FOX_HEADS):
            hs = slice(h * FOX_HD, (h + 1) * FOX_HD)
            s_h = jnp.sum(qc[hs, :] * kc_ref[0, hs, :], axis=0, keepdims=True)
            s_self = jnp.where(head_col == h, s_h, s_self)
        m_ref[...] = s_self
        acc_ref[...] = vc_ref[0]


def _decode_logits(qc_ref, kc_ref, vc_ref, fc_ref, bf_ref, k_refs, v_refs, lf_refs,
                   o_ref, lfo_ref, m_ref, l_ref, r_ref, acc_ref):
    rows = k_refs[0].shape[-1]
    qc = qc_ref[0] * (FOX_HD ** -0.5)
    lane = lax.broadcasted_iota(jnp.int32, (FOX_HEADS, rows), 1)
    head_row = lax.broadcasted_iota(jnp.int32, (FOX_HEADS, rows), 0)
    q_wide = [jnp.broadcast_to(qc[h * FOX_HD:(h + 1) * FOX_HD, :], (FOX_HD, rows)) for h in range(FOX_HEADS)]
    r_run = r_ref[...]
    s_pages = []
    for k_ref, lf_ref in zip(k_refs, lf_refs):
        lf = lf_ref[...]
        pre = lf
        sh = 1
        while sh < rows:
            pre = pre + jnp.where(lane >= sh, pltpu.roll(pre, sh, 1), 0.0)
            sh *= 2
        tot = pre[:, rows - 1:rows]
        bias = r_run + (tot - pre)
        r_run = r_run + tot
        s = jnp.zeros((FOX_HEADS, rows), F32)
        for h in range(FOX_HEADS):
            s_h = jnp.sum(q_wide[h] * k_ref[h], axis=0, keepdims=True)
            s = jnp.where(head_row == h, s_h, s)
        s_pages.append(s + bias)
    r_ref[...] = r_run

    m_old = m_ref[...]
    m_new = m_old
    for s in s_pages:
        m_new = jnp.maximum(m_new, jnp.max(s, axis=-1, keepdims=True))
    alpha = jnp.exp(m_old - m_new)
    m_ref[...] = m_new
    p_pages = [jnp.exp(s - m_new) for s in s_pages]
    p_sum = p_pages[0]
    for p in p_pages[1:]:
        p_sum = p_sum + p
    l_ref[...] = alpha * l_ref[...] + _lane_sum_mxu(p_sum)
    return alpha, p_pages


def _decode_values(alpha, p_pages, v_refs, acc_ref):
    parts = []
    for h in range(FOX_HEADS):
        acc = p_pages[0][h:h + 1, :] * v_refs[0][h]
        for p, v_ref in zip(p_pages[1:], v_refs[1:]):
            acc = acc + p[h:h + 1, :] * v_ref[h]
        parts.append(acc)
    acc_ref[...] = _per_head_rows(alpha) * acc_ref[...] + _lane_sum_mxu(jnp.concatenate(parts, axis=0))


def _decode_finish(step, n_steps, o_ref, l_ref, acc_ref):
    @pl.when(step == n_steps - 1)
    def _():
        o_ref[0] = acc_ref[...] * _per_head_rows(1.0 / l_ref[...])


N_DEC_IN = 8
N_DEC_SCRATCH = 8


def _decode_specs(arrs, seq_of, rows, g):
    per = lambda a: pl.BlockSpec((1,) + a.shape[1:], lambda *i: (seq_of(*i[:-1]),) + (0,) * (a.ndim - 1))
    qc, kc, vc, fc, bf_col = arrs
    in_specs = [per(qc), per(kc), per(vc), per(fc), pl.BlockSpec(bf_col.shape, lambda *i: (0, 0))]
    in_specs += [pl.BlockSpec(memory_space=pl.ANY)] * 3
    scratch = [pltpu.VMEM((FOX_HEADS, 1), F32), pltpu.VMEM((FOX_HEADS, 1), F32),
               pltpu.VMEM((FOX_HEADS, 1), F32), pltpu.VMEM((MIX_HALF, 1), F32),
               pltpu.VMEM((2, g, FOX_HEADS, FOX_HD, rows), F32), pltpu.VMEM((2, g, FOX_HEADS, FOX_HD, rows), F32),
               pltpu.VMEM((2, g, FOX_HEADS, rows), F32), pltpu.SemaphoreType.DMA((2, 3))]
    return in_specs, scratch


def _decode_begin(pt_ref, ins, outs, scratch, u, n_steps, seq_of, sps, g):
    m_ref, l_ref, r_ref, acc_ref, kbuf, vbuf, lfbuf, sems = scratch
    ck, cv, clf = ins[5:]
    n_pages = pt_ref.shape[1]

    def copies(step, slot, pages_known=True):
        out = []
        for pg in range(g):
            page = pt_ref[seq_of(step), n_pages - 1 - ((step % sps) * g + pg)] if pages_known else 0
            out.append(pltpu.make_async_copy(ck.at[page], kbuf.at[slot, pg], sems.at[slot, 0]))
            out.append(pltpu.make_async_copy(cv.at[page], vbuf.at[slot, pg], sems.at[slot, 1]))
            out.append(pltpu.make_async_copy(clf.at[page], lfbuf.at[slot, pg], sems.at[slot, 2]))
        return out

    slot = u % 2

    @pl.when(u == 0)
    def _():
        for c in copies(u, slot):
            c.start()

    for c in copies(jnp.minimum(u + 1, n_steps - 1), 1 - slot):
        c.start()
    for c in copies(u, slot, pages_known=False):
        c.wait()
    views = tuple([buf.at[slot, pg] for pg in range(g)] for buf in (kbuf, vbuf, lfbuf))
    dec = (*ins[:5], *views, *outs, m_ref, l_ref, r_ref, acc_ref)

    def drain():
        @pl.when(u == n_steps - 1)
        def _():
            for c in copies(u, 1 - slot, pages_known=False):
                c.wait()

    return dec, drain


def _decode_outs(n_seq, seq_of):
    shapes = [jax.ShapeDtypeStruct((n_seq, MIX_HALF, 1), F32), jax.ShapeDtypeStruct((n_seq, FOX_HEADS, 1), F32)]
    specs = [pl.BlockSpec((1, MIX_HALF, 1), lambda *i: (seq_of(*i[:-1]), 0, 0)),
             pl.BlockSpec((1, FOX_HEADS, 1), lambda *i: (seq_of(*i[:-1]), 0, 0))]
    return shapes, specs


def _tail_kernel(seq0, steps_per_seq, *refs):
    n_dec = steps_per_seq
    if n_dec:
        pt_ref, refs = refs[0], refs[1:]
    h1_ref, h2_ref, wo_ref, x_ref, g1_ref, b1_ref, wg_ref, wu_ref, wd_ref, g2_ref, b2_ref = refs[:11]
    h1n_ref, h2n_ref, xn_ref = refs[11:14]
    refs = refs[14:]
    if n_dec:
        dec_in, refs = refs[:N_DEC_IN], refs[N_DEC_IN:]
    o_ref, x1_ref = refs[0], refs[-1]
    half = h1_ref.shape[1]

    def mixed_in(h1, h2, x):
        mix = _dot(h1[...], wo_ref[0:half, :]) + _dot(h2[...], wo_ref[half:2 * half, :])
        return _layernorm(DN_ALPHA * x[...] + mix, g1_ref[...], b1_ref[...])

    @pl.when(pl.program_id(0) == 0)
    def _():
        x1_ref[...] = mixed_in(h1_ref, h2_ref, x_ref)

    if n_dec:
        u = pl.program_id(0)
        step = u % steps_per_seq
        dec, drain = _decode_begin(pt_ref, dec_in, refs[1:3], refs[3:-1], u, pl.num_programs(0),
                                   lambda s: seq0 + s // steps_per_seq, steps_per_seq, PAGES_PER_STEP)
        _decode_init(step, *dec)
    x1 = x1_ref[...]
    xb = x1.astype(BF16)
    gate = _dot(xb, wg_ref[...])
    up = _dot(xb, wu_ref[...])
    if n_dec:
        alpha, p_pages = _decode_logits(*dec)
    x1_next = mixed_in(h1n_ref, h2n_ref, xn_ref)
    if n_dec:
        _decode_values(alpha, p_pages, dec[6], dec[13])
    h = (_silu(gate) * up).astype(BF16)
    down = _dot(h, wd_ref[...])
    x1_ref[...] = x1_next
    o_ref[...] = _layernorm(DN_ALPHA * x1 + down, g2_ref[...], b2_ref[...])
    if n_dec:
        _decode_finish(step, steps_per_seq, dec[8], dec[11], dec[13])
        drain()


def _tail(h1, h1_col, h2, h2_col, wo, x2d, g1, b1, w_in, wd, layer, g2, b2, tm, dec=None):
    m, d = x2d.shape
    half = d // 2
    dff = wd.shape[1]
    nsteps = m // tm
    once = lambda shape, imap: pl.BlockSpec(shape, imap, pipeline_mode=pl.Buffered(1))
    in_specs = [pl.BlockSpec((tm, half), lambda i, *_: (i, h1_col)),
                pl.BlockSpec((tm, half), lambda i, *_: (i, h2_col)),
                once((d, d), lambda i, *_: (0, 0)),
                pl.BlockSpec((tm, d), lambda i, *_: (i, 0)),
                pl.BlockSpec((1, d), lambda i, *_: (0, 0)),
                pl.BlockSpec((1, d), lambda i, *_: (0, 0)),
                once((None, d, dff), lambda i, *_: (layer, 0, 0)),
                once((None, d, dff), lambda i, *_: (layer, 0, 1)),
                once((None, dff, d), lambda i, *_: (layer, 0, 0)),
                pl.BlockSpec((1, d), lambda i, *_: (0, 0)),
                pl.BlockSpec((1, d), lambda i, *_: (0, 0))]
    nxt = lambda i: jnp.minimum(i + 1, nsteps - 1)
    in_specs += [pl.BlockSpec((tm, half), lambda i, *_: (nxt(i), h1_col)),
                 pl.BlockSpec((tm, half), lambda i, *_: (nxt(i), h2_col)),
                 pl.BlockSpec((tm, d), lambda i, *_: (nxt(i), 0))]
    out_shape = [jax.ShapeDtypeStruct((m, d), F32)]
    out_specs = [pl.BlockSpec((tm, d), lambda i, *_: (i, 0))]
    x1_scratch = [pltpu.VMEM((tm, d), F32)]
    args = [h1, h2, wo, x2d, g1, b1, w_in, w_in, wd, g2, b2, h1, h2, x2d]
    if dec is None:
        return pl.pallas_call(
            functools.partial(_tail_kernel, 0, 0), out_shape=out_shape, grid=(nsteps,), in_specs=in_specs,
            out_specs=out_specs, scratch_shapes=x1_scratch,
            compiler_params=_cparams("arbitrary"), name="tail")(*args)[0]
    page_table, small, (ck, cv, clf), seq0, n_seq = dec
    n_pages = page_table.shape[1]
    g = PAGES_PER_STEP
    sps = n_pages // g
    assert nsteps == n_seq * sps
    seq_of = lambda i: seq0 + i // sps
    dec_specs, scratch = _decode_specs(small, seq_of, ck.shape[3], g)
    d_shapes, d_specs = _decode_outs(n_seq, lambda i: i // sps)
    grid_spec = pltpu.PrefetchScalarGridSpec(
        num_scalar_prefetch=1, grid=(nsteps,), in_specs=in_specs + dec_specs, out_specs=out_specs + d_specs,
        scratch_shapes=scratch + x1_scratch)
    return pl.pallas_call(
        functools.partial(_tail_kernel, seq0, sps), out_shape=out_shape + d_shapes, grid_spec=grid_spec,
        compiler_params=_cparams("arbitrary"), name="tail_decode",
    )(page_table, *args, *small, ck, cv, clf)


def _gla_level_masks():
    c = GLA_CHUNK
    t = np.arange(c)[:, None]
    s = np.arange(c)[None, :]
    x = t ^ s
    masks = [x == 0] + [(t > s) & (x >= (1 << (l - 1))) & (x < (1 << l)) for l in range(1, GLA_LEVELS + 1)]
    m = np.stack(masks).astype(np.float32)
    return jnp.asarray(np.concatenate([m, m], axis=2))


def _gla_sum_matrices():
    c = GLA_CHUNK
    t = np.arange(c)[:, None]
    u = np.arange(c)[None, :]
    mq = [(u <= t)]
    mk = [(u > t)]
    for l in range(1, GLA_LEVELS + 1):
        blk, half = 1 << l, 1 << (l - 1)
        same = (t // blk) == (u // blk)
        mq.append(same & (t % blk >= half) & (u % blk >= half) & (u <= t))
        mk.append(same & (t % blk < half) & (u % blk < half) & (u > t))
    mk.append(np.ones((c, c), bool))
    mq = np.concatenate(mq, axis=0).astype(np.float32)
    mk = np.concatenate(mk, axis=0).astype(np.float32).T
    mq2 = np.concatenate([mq, mq], axis=1)
    mk2 = np.concatenate([mk, mk], axis=0)
    return jnp.asarray(mq2, BF16), jnp.asarray(mk2, BF16)


def _split_hi_lo(x):
    hi = x.astype(BF16)
    lo = (x - hi.astype(F32)).astype(BF16)
    return hi, lo


def _mix0_kernel(bg_ref, cg_ref, hv_ref, v_ref, gt_ref, q_ref, r_ref, kt_ref, rt_ref,
                 convw_ref, w2_ref, w2t_ref, bg2_ref, bg2c_ref, ng_ref, mq_ref, mk_ref, lvl_ref,
                 mixed_ref, convst_ref, glast_ref, s_ref, ext_ref):
    c = GLA_CHUNK
    rows = bg_ref.shape[0]
    i = pl.program_id(1)

    @pl.when(i == 0)
    def _():
        s_ref[...] = jnp.zeros_like(s_ref)
        ext_ref[0:8, :] = jnp.zeros((8, MIX_HALF), F32)

    u = cg_ref[...] * hv_ref[...]
    ext_ref[8:8 + rows, :] = u
    w = convw_ref[...]
    y = w[2:3, :] * u + w[1:2, :] * ext_ref[7:7 + rows, :] + w[0:1, :] * ext_ref[6:6 + rows, :]
    mixed_ref[:, 0:MIX_HALF] = (bg_ref[...] * y).astype(BF16)
    ext_ref[0:8, :] = u[rows - 8:rows, :]
    convst_ref[0] = u[rows - (SC_K - 1):rows, :]

    pair = 2 * GLA_DK
    own_block = ((lax.broadcasted_iota(jnp.int32, (pair, 2 * c), 0) < GLA_DK)
                 == (lax.broadcasted_iota(jnp.int32, (pair, 2 * c), 1) < c))

    def pair_dot(qp, ktp):
        kbd = jnp.where(own_block, jnp.concatenate([ktp, ktp], axis=1), 0.0).astype(BF16)
        return _dot(qp.astype(BF16), kbd)

    chunks = []
    for n in range(rows // c):
        rs = slice(n * c, (n + 1) * c)
        z = _dot(r_ref[rs, :].astype(BF16), w2_ref[...]) + bg2_ref[...]
        la = _log_sigmoid(z) * GLA_INV_TAU
        zt = _dot(w2t_ref[...], rt_ref[0, :, rs].astype(BF16)) + bg2c_ref[...]
        lat = _log_sigmoid(zt) * GLA_INV_TAU
        hi, lo = _split_hi_lo(la)
        eq = _dot(mq_ref[...], jnp.concatenate([hi, lo], axis=0))
        hit, lot = _split_hi_lo(lat)
        ek = _dot(jnp.concatenate([hit, lot], axis=1), mk_ref[...])
        chunks.append((rs, q_ref[rs, :] * (GLA_DK ** -0.5), kt_ref[0, :, rs], eq, ek))

    all_scores = []
    for rs, q, kt, eq, ek in chunks:
        scores = []
        for hp in range(GLA_HEADS // 2):
            ps = slice(hp * pair, (hp + 1) * pair)
            qp = q[:, ps]
            ktp = kt[ps, :]
            a = lvl_ref[0] * pair_dot(qp, ktp)
            for l in range(1, GLA_LEVELS + 1):
                ql = qp * jnp.exp(eq[l * c:(l + 1) * c, ps])
                kl = ktp * jnp.exp(ek[ps, l * c:(l + 1) * c])
                a = a + lvl_ref[l] * pair_dot(ql, kl)
            a = a.astype(BF16)
            scores += [a[:, 0:c], a[:, c:2 * c]]
        all_scores.append(scores)

    for (rs, q, kt, eq, ek), scores in zip(chunks, all_scores):
        for h in range(GLA_HEADS):
            ks = slice(h * GLA_DK, (h + 1) * GLA_DK)
            vs = slice(h * GLA_DV, (h + 1) * GLA_DV)
            qh = q[:, ks]
            kth = kt[ks, :]
            vh = v_ref[rs, vs].astype(BF16)
            s_old = s_ref[h]
            q_read = (qh * jnp.exp(eq[0:c, ks])).astype(BF16)
            o = _dot(scores[h], vh) + _dot(q_read, s_old.astype(BF16))
            k_write = (kth * jnp.exp(ek[ks, 0:c])).astype(BF16)
            a_chunk = jnp.exp(ek[ks, (GLA_LEVELS + 1) * c:(GLA_LEVELS + 2) * c])
            s_ref[h] = a_chunk * s_old + _dot(k_write, vh)
            on = o * lax.rsqrt(jnp.mean(o * o, axis=-1, keepdims=True) + EPS) * ng_ref[...]
            mixed_ref[rs, MIX_HALF + h * GLA_DV:MIX_HALF + (h + 1) * GLA_DV] = (
                on * _silu(gt_ref[rs, vs])).astype(BF16)
    glast_ref[0] = s_ref[...]


def _mix0(pn, kt, rt, conv_w, w2, w2t, bg2, bg2c, ng, nb):
    m = pn.shape[0]
    l = m // nb
    c = min(l, MIX_ROWS)
    assert c % GLA_CHUNK == 0 and l % c == 0
    nl = l // c
    mq, mk = _gla_sum_matrices()
    lvl = _gla_level_masks()
    row = lambda b, i: b * nl + i
    full = lambda a: pl.BlockSpec(a.shape, lambda b, i: (0,) * a.ndim)
    wide = lambda col: pl.BlockSpec((c, MIX_HALF), lambda b, i: (row(b, i), col))
    q_col = 5 * MIX_HALF // (GLA_HEADS * GLA_DK)
    r_col = (5 * MIX_HALF + GLA_HEADS * GLA_DK) // R_PAD
    return pl.pallas_call(
        _mix0_kernel,
        out_shape=[jax.ShapeDtypeStruct((m, D_MODEL), BF16),
                   jax.ShapeDtypeStruct((nb, SC_K - 1, MIX_HALF), F32),
                   jax.ShapeDtypeStruct((nb, GLA_HEADS, GLA_DK, GLA_DV), F32)],
        grid=(nb, nl),
        in_specs=[wide(0), wide(1), wide(2), wide(3), wide(4),
                  pl.BlockSpec((c, GLA_HEADS * GLA_DK), lambda b, i: (row(b, i), q_col)),
                  pl.BlockSpec((c, R_PAD), lambda b, i: (row(b, i), r_col)),
                  pl.BlockSpec((1, GLA_HEADS * GLA_DK, c), lambda b, i: (b, 0, i)),
                  pl.BlockSpec((1, GLA_RANK, c), lambda b, i: (b, 0, i)),
                  full(conv_w), full(w2), full(w2t), full(bg2), full(bg2c), full(ng), full(mq), full(mk),
                  full(lvl)],
        out_specs=[pl.BlockSpec((c, D_MODEL), lambda b, i: (row(b, i), 0)),
                   pl.BlockSpec((1, SC_K - 1, MIX_HALF), lambda b, i: (b, 0, 0)),
                   pl.BlockSpec((1, GLA_HEADS, GLA_DK, GLA_DV), lambda b, i: (b, 0, 0, 0))],
        scratch_shapes=[pltpu.VMEM((GLA_HEADS, GLA_DK, GLA_DV), F32), pltpu.VMEM((c + 8, MIX_HALF), F32)],
        compiler_params=_cparams("parallel", "arbitrary"),
        name="mix0",
    )(pn, pn, pn, pn, pn, pn, pn, kt, rt, conv_w, w2, w2t, bg2, bg2c, ng, mq, mk, lvl)


C_ROWS = 16
LOG2E = 1.4426950408889634


def _fox_c_kernel(ft_ref, bf_ref, lf_ref, c_ref):
    lf = _log_sigmoid(ft_ref[0] + bf_ref[...])
    lf_ref[0] = lf
    n = lf.shape[1]
    lane = lax.broadcasted_iota(jnp.int32, lf.shape, 1)
    suf = lf
    s = 1
    while s < n:
        suf = suf + jnp.where(lane + s < n, pltpu.roll(suf, n - s, 1), 0.0)
        s *= 2
    bias = (suf - lf) * LOG2E
    hi = bias.astype(BF16).astype(F32)
    r1 = bias - hi
    mid = r1.astype(BF16).astype(F32)
    lo = (r1 - mid).astype(BF16).astype(F32)
    row = lax.broadcasted_iota(jnp.int32, (C_ROWS, n), 0)
    for h in range(FOX_HEADS):
        hs = slice(h, h + 1)
        parts = jnp.where(row == 0, hi[hs], jnp.where(row == 1, mid[hs], jnp.where(row == 2, lo[hs], 0.0)))
        c_ref[0, h] = parts.astype(BF16)


def _fox_c(ft, bf_col):
    nb, _, l = ft.shape
    spec = pl.BlockSpec((1, FOX_HEADS, l), lambda b: (b, 0, 0))
    return pl.pallas_call(
        _fox_c_kernel,
        out_shape=[jax.ShapeDtypeStruct((nb, FOX_HEADS, l), F32),
                   jax.ShapeDtypeStruct((nb, FOX_HEADS, C_ROWS, l), BF16)],
        grid=(nb,),
        in_specs=[spec, pl.BlockSpec((FOX_HEADS, 1), lambda b: (0, 0))],
        out_specs=[spec, pl.BlockSpec((1, FOX_HEADS, C_ROWS, l), lambda b: (b, 0, 0, 0))],
        compiler_params=_cparams("parallel"),
        name="fox_c",
    )(ft, bf_col)


AUG = 2 * FOX_HD
FLASH_AHEAD = 8


def _fox_flash_kernel(dec_pages, steps_per_seq, seq0, n_seq, *refs):
    sched_ref = refs[0]
    if dec_pages:
        pt_ref = refs[1]
        refs = refs[2:]
    else:
        refs = refs[1:]
    q_ref, kt_ref, vt_ref, c_ref = refs[:4]
    refs = refs[4:]
    t = pl.program_id(0)
    if dec_pages:
        dec_in, refs = refs[:N_DEC_IN], refs[N_DEC_IN:]
        o_ref, do_ref, dlf_ref, qa_ref, m_ref, acc_ref = refs[:6]
        seq_of = lambda s: seq0 + jnp.minimum(s // steps_per_seq, n_seq - 1)
        dec, drain = _decode_begin(pt_ref, dec_in, (do_ref, dlf_ref), refs[6:], t, pl.num_programs(0),
                                   seq_of, steps_per_seq, dec_pages)
        dacc_ref = dec[13]
    else:
        o_ref, qa_ref, m_ref, acc_ref = refs
    qi = sched_ref[1, t]
    kj = sched_ref[2, t]
    tq = q_ref.shape[0]
    tk = kt_ref.shape[2]
    lane = lax.broadcasted_iota(jnp.int32, (tq, AUG), 1)
    if dec_pages:
        dstep = t % steps_per_seq
        _decode_init(dstep, *dec)

    @pl.when(kj == 0)
    def _():
        ones3 = jnp.where(lane < FOX_HD + 3, 1.0, 0.0)
        for p in range(FOX_HEADS // 2):
            qp = q_ref[:, p * AUG:(p + 1) * AUG] * (FOX_HD ** -0.5 * LOG2E)
            qa_ref[2 * p] = jnp.where(lane < FOX_HD, qp, ones3).astype(BF16)
            qa_ref[2 * p + 1] = jnp.where(lane < FOX_HD, pltpu.roll(qp, FOX_HD, 1), ones3).astype(BF16)
        m_ref[...] = jnp.full_like(m_ref, -jnp.inf)
        acc_ref[...] = jnp.zeros_like(acc_ref)

    def block(masked):
        if masked:
            visible = (lax.broadcasted_iota(jnp.int32, (tq, tk), 1) <= lax.broadcasted_iota(jnp.int32, (tq, tk), 0))
        pad_k = jnp.zeros((AUG - FOX_HD - C_ROWS, tk), BF16)
        ones_row = jnp.where(lax.broadcasted_iota(jnp.int32, (AUG - FOX_HD, tk), 0) == 0, 1.0, 0.0).astype(BF16)

        def logits(h):
            hs = slice(h * FOX_HD, (h + 1) * FOX_HD)
            ka = jnp.concatenate([kt_ref[0, hs, :], c_ref[0, h], pad_k], axis=0)
            return _dot(qa_ref[h], ka)

        s_ahead = [logits(h) for h in range(FLASH_AHEAD)]
        if dec_pages:
            alpha_d, p_d = _decode_logits(*dec)
        for h in range(FOX_HEADS):
            if dec_pages and h == FOX_HEADS // 2:
                _decode_values(alpha_d, p_d, dec[6], dacc_ref)
            if h + FLASH_AHEAD < FOX_HEADS:
                s_ahead.append(logits(h + FLASH_AHEAD))
            s = s_ahead.pop(0)
            hs = slice(h * FOX_HD, (h + 1) * FOX_HD)
            va = jnp.concatenate([vt_ref[0, hs, :], ones_row], axis=0)
            if masked:
                s = jnp.where(visible, s, -jnp.inf)
            m_old = m_ref[h]
            m_new = jnp.maximum(m_old, jnp.max(s, axis=-1, keepdims=True))
            pr = jnp.exp2(s - jnp.concatenate([m_new] * (tk // AUG), axis=1)).astype(BF16)
            acc_ref[h] = jnp.exp2(m_old - m_new) * acc_ref[h] + _dot_nt(pr, va)
            m_ref[h] = m_new

    @pl.when(kj < qi)
    def _():
        block(False)

    @pl.when(kj == qi)
    def _():
        block(True)
        for p in range(FOX_HEADS // 2):
            a0 = acc_ref[2 * p]
            a1 = acc_ref[2 * p + 1]
            o0 = a0 * (1.0 / a0[:, FOX_HD:FOX_HD + 1])
            o1 = a1 * (1.0 / a1[:, FOX_HD:FOX_HD + 1])
            o_ref[:, p * AUG:(p + 1) * AUG] = jnp.where(lane < FOX_HD, o0, pltpu.roll(o1, FOX_HD, 1)).astype(BF16)

    if dec_pages:
        _decode_finish(dstep, steps_per_seq, do_ref, dec[11], dacc_ref)
        drain()


def _fox_flash(pn, kt, vt, caug, nb, tq, dec=None):
    m = pn.shape[0]
    l = m // nb
    nq = l // tq
    sched = jnp.asarray(np.array([(b, i, j) for b in range(nb) for i in range(nq) for j in range(i + 1)],
                                 np.int32).T)
    steps = sched.shape[1]
    kv_spec = pl.BlockSpec((1, MIX_HALF, tq), lambda t, sc, *_: (sc[0, t], 0, sc[2, t]))
    q_map = lambda t, sc, *_: (sc[0, t] * nq + sc[1, t], 0)
    in_specs = [pl.BlockSpec((tq, MIX_HALF), q_map), kv_spec, kv_spec,
                pl.BlockSpec((1, FOX_HEADS, C_ROWS, tq), lambda t, sc, *_: (sc[0, t], 0, 0, sc[2, t]))]
    out_shape = [jax.ShapeDtypeStruct((m, MIX_HALF), BF16)]
    out_specs = [pl.BlockSpec((tq, MIX_HALF), q_map)]
    scratch = [pltpu.VMEM((FOX_HEADS, tq, AUG), BF16), pltpu.VMEM((FOX_HEADS, tq, AUG), F32),
               pltpu.VMEM((FOX_HEADS, tq, AUG), F32)]
    if dec is None:
        grid_spec = pltpu.PrefetchScalarGridSpec(
            num_scalar_prefetch=1, grid=(steps,), in_specs=in_specs, out_specs=out_specs, scratch_shapes=scratch)
        return pl.pallas_call(
            functools.partial(_fox_flash_kernel, 0, 0, 0, 0), out_shape=out_shape, grid_spec=grid_spec,
            compiler_params=_cparams("arbitrary"), name="fox_flash")(sched, pn, kt, vt, caug)[0]
    page_table, small, (ck, cv, clf), seq0, n_seq, g = dec
    n_pages = page_table.shape[1]
    sps = n_pages // g
    slots = steps // sps
    assert steps % sps == 0 and slots >= n_seq
    seq_of = lambda t, *_: seq0 + jnp.minimum(t // sps, n_seq - 1)
    dec_specs, dec_scratch = _decode_specs(small, seq_of, ck.shape[3], g)
    d_shapes, d_specs = _decode_outs(slots, lambda t, *_: t // sps)
    grid_spec = pltpu.PrefetchScalarGridSpec(
        num_scalar_prefetch=2, grid=(steps,), in_specs=in_specs + dec_specs, out_specs=out_specs + d_specs,
        scratch_shapes=scratch + dec_scratch)
    o_c, o_part, lf_part = pl.pallas_call(
        functools.partial(_fox_flash_kernel, g, sps, seq0, n_seq), out_shape=out_shape + d_shapes,
        grid_spec=grid_spec, compiler_params=_cparams("arbitrary"), name="fox_flash_decode",
    )(sched, page_table, pn, kt, vt, caug, *small, ck, cv, clf)
    return o_c, o_part[:n_seq], lf_part[:n_seq]


CF_HALO = 32
SUBLANES = 8


CONF_ROWS = 256


def _conf_rows(a, gate, w_ref, cb_ref, g_ref, b_ref, ext_ref):
    t = a.shape[0]
    u = a * _sigmoid(gate)
    ext_ref[CF_HALO:CF_HALO + t, :] = u
    off = CF_HALO - (CF_K - 1)
    acc = jnp.zeros((t, MIX_HALF), F32) + cb_ref[...]
    for rho in range(SUBLANES):
        taps = [j for j in range(CF_K) if (off + j) % SUBLANES == rho]
        rows = t if rho == 0 else t + SUBLANES
        part = jnp.zeros((rows, MIX_HALF), F32)
        for j in taps:
            a0 = off + j - rho
            part = part + w_ref[j:j + 1, :] * ext_ref[a0:a0 + rows, :]
        acc = acc + part[rho:rho + t, :]
    ext_ref[0:CF_HALO, :] = ext_ref[t:t + CF_HALO, :]
    wide = lambda r: jnp.concatenate([r] * (MIX_HALF // LANES), axis=1)
    d = acc - wide(_lane_sum_mxu_wide(acc) * (1.0 / MIX_HALF))
    inv = lax.rsqrt(_lane_sum_mxu_wide(d * d) * (1.0 / MIX_HALF) + EPS)
    return _silu(d * wide(inv) * g_ref[...] + b_ref[...]).astype(BF16)


def _proj_conf_kernel(t_outs, x_ref, wn_ref, wt_ref, w_ref, cb_ref, g_ref, b_ref, q_ref, y_ref, st_ref, *rest):
    pt_refs, ext_ref = rest[:-1], rest[-1]
    tl = x_ref.shape[0]
    sub = min(tl, CONF_ROWS)

    @pl.when(pl.program_id(1) == 0)
    def _():
        ext_ref[0:CF_HALO, :] = jnp.zeros((CF_HALO, MIX_HALF), F32)

    xb = x_ref[...].astype(BF16)
    n_groups = tl // sub
    group = lambda n: _dot(xb[n * sub:(n + 1) * sub], wn_ref[...])
    pn_next = group(0)
    pt = None
    for n in range(n_groups):
        pn = pn_next
        if n + 1 < n_groups:
            pn_next = group(n + 1)
        else:
            pt = _dot_nt(wt_ref[...], xb)
        rs = slice(n * sub, (n + 1) * sub)
        q_ref[rs, :] = pn[:, 0:MIX_HALF]
        y_ref[rs, :] = _conf_rows(pn[:, MIX_HALF:2 * MIX_HALF], pn[:, 2 * MIX_HALF:3 * MIX_HALF],
                                  w_ref, cb_ref, g_ref, b_ref, ext_ref)
    st_ref[0] = ext_ref[CF_HALO - (CF_K - 1):CF_HALO, :]
    for ref, (r0, n, dt) in zip(pt_refs, t_outs):
        ref[0] = pt[r0:r0 + n].astype(dt)


def _proj_conf(x2d, wn, wt, t_outs, w, cb, g, b, nb, tl):
    m, k = x2d.shape
    l = m // nb
    nl = l // tl
    full = lambda a: pl.BlockSpec(a.shape, lambda bb, i: (0,) * a.ndim)
    rows = lambda bb, i: (bb * nl + i, 0)
    out_shape = [jax.ShapeDtypeStruct((m, MIX_HALF), F32), jax.ShapeDtypeStruct((m, MIX_HALF), BF16),
                 jax.ShapeDtypeStruct((nb, CF_K - 1, MIX_HALF), F32)]
    out_shape += [jax.ShapeDtypeStruct((nb, n, l), dt) for _, n, dt in t_outs]
    out_specs = [pl.BlockSpec((tl, MIX_HALF), rows), pl.BlockSpec((tl, MIX_HALF), rows),
                 pl.BlockSpec((1, CF_K - 1, MIX_HALF), lambda bb, i: (bb, 0, 0))]
    out_specs += [pl.BlockSpec((1, n, tl), lambda bb, i: (bb, 0, i)) for _, n, _ in t_outs]
    return pl.pallas_call(
        functools.partial(_proj_conf_kernel, t_outs),
        out_shape=out_shape,
        grid=(nb, nl),
        in_specs=[pl.BlockSpec((tl, k), rows), full(wn), full(wt), full(w), full(cb), full(g), full(b)],
        out_specs=out_specs,
        scratch_shapes=[pltpu.VMEM((min(tl, CONF_ROWS) + CF_HALO, MIX_HALF), F32)],
        compiler_params=_cparams("parallel", "arbitrary"),
        name="proj_conf",
    )(x2d, wn, wt, w, cb, g, b)


def _smix0_kernel(pn_ref, kc_ref, qc_ref, rc_ref, prev_ref, s_ref, convw_ref, w2t_ref, bg2c_ref, ng_ref,
                  mixed_ref, convst_ref, sout_ref):
    w = convw_ref[...]
    w2t = w2t_ref[...]
    for i in range(pn_ref.shape[0]):
        pn = pn_ref[i]
        bg, cg, hv = pn[:, 0:MIX_HALF], pn[:, MIX_HALF:2 * MIX_HALF], pn[:, 2 * MIX_HALF:3 * MIX_HALF]
        v, gt = pn[:, 3 * MIX_HALF:4 * MIX_HALF], pn[:, 4 * MIX_HALF:5 * MIX_HALF]
        u = cg * hv
        prev = prev_ref[i]
        y = w[0:1, :] * prev[0:1, :] + w[1:2, :] * prev[1:2, :] + w[2:3, :] * u
        mixed_ref[i, :, 0:MIX_HALF] = (bg * y).astype(BF16)
        convst_ref[i, 0:1, :] = prev[1:2, :]
        convst_ref[i, 1:2, :] = u

        zc = _dot(w2t, rc_ref[i].astype(BF16)) + bg2c_ref[...]
        ac = jnp.exp(_log_sigmoid(zc) * GLA_INV_TAU)
        qc = qc_ref[i] * (GLA_DK ** -0.5)
        kc = kc_ref[i]
        for h in range(GLA_HEADS):
            ks = slice(h * GLA_DK, (h + 1) * GLA_DK)
            vs = slice(h * GLA_DV, (h + 1) * GLA_DV)
            s_new = ac[ks, :] * s_ref[i, h] + kc[ks, :] * v[:, vs]
            sout_ref[i, h] = s_new
            o = jnp.sum(qc[ks, :] * s_new, axis=0, keepdims=True)
            on = o * lax.rsqrt(jnp.mean(o * o, axis=-1, keepdims=True) + EPS) * ng_ref[...]
            mixed_ref[i, :, MIX_HALF + h * GLA_DV:MIX_HALF + (h + 1) * GLA_DV] = (
                on * _silu(gt[:, vs])).astype(BF16)


def _smix0(pn3, kc, qc, rc, prev, s0, conv_w, w2t, bg2c, ng):
    nd, _, nn = pn3.shape
    sb = _tile_rows(nd, SAMPLE_SEQS_PER_STEP)
    per = lambda a: pl.BlockSpec((sb,) + a.shape[1:], lambda b: (b,) + (0,) * (a.ndim - 1))
    full = lambda a: pl.BlockSpec(a.shape, lambda b: (0,) * a.ndim)
    out_shape = [jax.ShapeDtypeStruct((nd, 1, D_MODEL), BF16),
                 jax.ShapeDtypeStruct((nd, SC_K - 1, MIX_HALF), F32),
                 jax.ShapeDtypeStruct(s0.shape, F32)]
    return pl.pallas_call(
        _smix0_kernel,
        out_shape=out_shape,
        grid=(nd // sb,),
        in_specs=[per(pn3), per(kc), per(qc), per(rc), per(prev), per(s0),
                  full(conv_w), full(w2t), full(bg2c), full(ng)],
        out_specs=[per(o) for o in out_shape],
        compiler_params=_cparams("parallel"),
        name="smix0",
    )(pn3, kc, qc, rc, prev, s0, conv_w, w2t, bg2c, ng)


def _sconf_kernel(pn_ref, prev_ref, w_ref, cb_ref, g_ref, b_ref, y_ref, st_ref):
    w = w_ref[...]
    for i in range(pn_ref.shape[0]):
        pn = pn_ref[i]
        a, gate = pn[:, MIX_HALF:2 * MIX_HALF], pn[:, 2 * MIX_HALF:3 * MIX_HALF]
        u = a * _sigmoid(gate)
        prev = prev_ref[i]
        acc = jnp.sum(w[0:CF_K - 1, :] * prev, axis=0, keepdims=True) + w[CF_K - 1:CF_K, :] * u + cb_ref[...]
        y_ref[i] = _silu(_layernorm(acc, g_ref[...], b_ref[...])).astype(BF16)
        st_ref[i, 0:CF_K - 2, :] = prev_ref[i, 1:CF_K - 1, :]
        st_ref[i, CF_K - 2:CF_K - 1, :] = u


def _sconf(pn3, prev, w, cb, g, b):
    nd = pn3.shape[0]
    sb = _tile_rows(nd, SAMPLE_SEQS_PER_STEP)
    per = lambda a: pl.BlockSpec((sb,) + a.shape[1:], lambda bb: (bb,) + (0,) * (a.ndim - 1))
    full = lambda a: pl.BlockSpec(a.shape, lambda bb: (0,) * a.ndim)
    out_shape = [jax.ShapeDtypeStruct((nd, 1, MIX_HALF), BF16), jax.ShapeDtypeStruct(prev.shape, F32)]
    return pl.pallas_call(
        _sconf_kernel,
        out_shape=out_shape,
        grid=(nd // sb,),
        in_specs=[per(pn3), per(prev), full(w), full(cb), full(g), full(b)],
        out_specs=[per(o) for o in out_shape],
        compiler_params=_cparams("parallel"),
        name="sconf",
    )(pn3, prev, w, cb, g, b)


def _decode_kernel(seq0, steps_per_seq, pt_ref, *refs):
    u = pl.program_id(0)
    step = u % steps_per_seq
    dec, drain = _decode_begin(pt_ref, refs[:N_DEC_IN], refs[N_DEC_IN:N_DEC_IN + 2], refs[N_DEC_IN + 2:], u,
                               pl.num_programs(0), lambda s: seq0 + s // steps_per_seq, steps_per_seq,
                               PAGES_PER_STEP)
    _decode_init(step, *dec)
    alpha, p_pages = _decode_logits(*dec)
    _decode_values(alpha, p_pages, dec[6], dec[13])
    _decode_finish(step, steps_per_seq, dec[8], dec[11], dec[13])
    drain()


def _decode(page_table, small, caches, seq0, n_seq):
    ck, cv, clf = caches
    g = PAGES_PER_STEP
    sps = page_table.shape[1] // g
    in_specs, scratch = _decode_specs(small, lambda u: seq0 + u // sps, ck.shape[3], g)
    out_shape, out_specs = _decode_outs(n_seq, lambda u: u // sps)
    grid_spec = pltpu.PrefetchScalarGridSpec(
        num_scalar_prefetch=1, grid=(n_seq * sps,), in_specs=in_specs, out_specs=out_specs,
        scratch_shapes=scratch)
    return pl.pallas_call(
        functools.partial(_decode_kernel, seq0, sps),
        out_shape=out_shape,
        grid_spec=grid_spec,
        compiler_params=_cparams("arbitrary"),
        name="fox_decode",
    )(page_table, *small, ck, cv, clf)


def _tile_rows(n, pref):
    t = min(n, pref)
    while n % t:
        t //= 2
    return t


def kernel(x_prompt, x_sample, state_conv_a, state_gla, cache_k, cache_v, cache_logf, state_conv_d, page_table,
           ab_w_in, ab_conv_w, gla_w_gate2, gla_b_gate, gla_norm_g, ab_w_out,
           cd_w_in, fox_b_f, cf_conv_w, cf_conv_b, cf_ln_g, cf_ln_b, cd_w_out,
           ffn_w_in, ffn_w_out, ln_g, ln_b):
    nb, seq, d = x_prompt.shape
    nd = x_sample.shape[0]
    hk = GLA_HEADS * GLA_DK

    o_bg, o_cg, o_hv = 0, MIX_HALF, 2 * MIX_HALF
    o_q = 3 * MIX_HALF
    o_k = o_q + hk
    o_v = o_k + hk
    o_g = o_v + MIX_HALF
    o_r = o_g + MIX_HALF
    wab = ab_w_in.astype(BF16)
    ab_wn = jnp.concatenate([wab[:, o_bg:o_q], wab[:, o_v:o_g], wab[:, o_g:o_r], wab[:, o_q:o_k],
                             jnp.pad(wab[:, o_r:o_r + GLA_RANK], ((0, 0), (0, R_PAD - GLA_RANK)))], axis=1)
    wab_t = wab.T
    ab_wt_p = jnp.concatenate([wab_t[o_k:o_v], wab_t[o_r:o_r + GLA_RANK]], axis=0)
    ab_wt_s = jnp.concatenate([ab_wt_p, wab_t[o_q:o_k]], axis=0)
    w2 = jnp.pad(gla_w_gate2.astype(BF16), ((0, R_PAD - GLA_RANK), (0, 0)))
    w2t = gla_w_gate2.T.astype(BF16)
    bg2 = gla_b_gate.reshape(1, hk)
    bg2c = gla_b_gate.reshape(hk, 1)
    ng = gla_norm_g.reshape(1, GLA_DV)
    ab_wo = ab_w_out.astype(BF16)

    wcd = cd_w_in.astype(BF16)
    c_q, c_k, c_v = 0, MIX_HALF, 2 * MIX_HALF
    c_f = 3 * MIX_HALF
    c_a = c_f + FOX_HEADS
    c_gate = c_a + MIX_HALF
    cd_wn = jnp.concatenate([wcd[:, c_q:c_k], wcd[:, c_a:c_gate], wcd[:, c_gate:c_gate + MIX_HALF]], axis=1)
    cd_wn_s = jnp.concatenate([cd_wn, wcd[:, c_k:c_f]], axis=1)
    wcd_t = wcd.T
    cd_wt_p = jnp.concatenate([wcd_t[c_k:c_v], wcd_t[c_v:c_f],
                               jnp.pad(wcd_t[c_f:c_a], ((0, F_PAD - FOX_HEADS), (0, 0)))], axis=0)
    cd_wt_s = jnp.concatenate([cd_wt_p, wcd_t[c_q:c_k]], axis=0)
    bf_col = fox_b_f.reshape(FOX_HEADS, 1)
    cb = cf_conv_b.reshape(1, MIX_HALF)
    cg_ = cf_ln_g.reshape(1, MIX_HALF)
    cbb = cf_ln_b.reshape(1, MIX_HALF)
    cd_wo = cd_w_out.astype(BF16)
    ffn_wi = ffn_w_in.astype(BF16)
    ffn_wo = ffn_w_out.astype(BF16)

    def tail(x2d, h1, c1, h2, c2, wo, layer, tm, dec=None):
        return _tail(h1, c1, h2, c2, wo, x2d, ln_g[layer, 0:1], ln_b[layer, 0:1], ffn_wi, ffn_wo, layer,
                     ln_g[layer, 1:2], ln_b[layer, 1:2], tm, dec)

    xs = x_sample.reshape(nd, d)
    pn_s, kc_s, rc_s, qc_s = _proj(xs, ab_wn, ab_wt_s,
                                   ((0, hk, F32), (hk, GLA_RANK, F32), (hk + GLA_RANK, hk, F32)), 1, nd,
                                   as_columns=True)
    mixed_s, conv_a_s, gla_s = _smix0(pn_s.reshape(nd, 1, -1), kc_s, qc_s, rc_s,
                                      state_conv_a, state_gla, ab_conv_w, w2t, bg2c, ng)
    mixed_s = mixed_s.reshape(nd, d)
    xs2 = tail(xs, mixed_s, 0, mixed_s, 1, ab_wo, 0, nd)
    kvf = ((0, MIX_HALF, F32), (MIX_HALF, MIX_HALF, F32), (2 * MIX_HALF, F_PAD, F32))
    cols_s = ((0, MIX_HALF, F32), (MIX_HALF, MIX_HALF, F32), (2 * MIX_HALF, FOX_HEADS, F32),
              (2 * MIX_HALF + F_PAD, MIX_HALF, F32))
    pn1_s, kc1_s, vc1_s, fc1_s, qc1_s = _proj(xs2, cd_wn_s, cd_wt_s, cols_s, 1, nd, as_columns=True)
    caches = (jnp.transpose(cache_k, (0, 2, 3, 1)), jnp.transpose(cache_v, (0, 2, 3, 1)),
              jnp.transpose(cache_logf, (0, 2, 1)))
    dec_small = (qc1_s, kc1_s, vc1_s, fc1_s, bf_col)

    m = nb * seq
    tm = _tile_rows(m, TAIL_ROWS)
    tl = _tile_rows(seq, SEQ_TILE)
    n_pages = page_table.shape[1]
    per_tail = 0
    if n_pages % PAGES_PER_STEP == 0 and (m // tm) % (n_pages // PAGES_PER_STEP) == 0:
        per_tail = (m // tm) // (n_pages // PAGES_PER_STEP)
        if DEPTH * per_tail > nd:
            per_tail = 0
    in_tails = DEPTH * per_tail
    flash_steps = nb * (seq // tl) * (seq // tl + 1) // 2
    in_flash = 0
    if n_pages % FLASH_PAGES == 0 and flash_steps % (n_pages // FLASH_PAGES) == 0:
        in_flash = max(0, min(flash_steps // (n_pages // FLASH_PAGES), nd - in_tails))
    dec_parts = {}

    def prompt_tail(x2d, h1, c1, h2, c2, wo, layer):
        if per_tail == 0:
            return tail(x2d, h1, c1, h2, c2, wo, layer, tm)
        y, o_part, lf_part = tail(x2d, h1, c1, h2, c2, wo, layer, tm,
                                  (page_table, dec_small, caches, layer * per_tail, per_tail))
        dec_parts[layer * per_tail] = (o_part, lf_part)
        return y

    x0 = x_prompt.reshape(m, d)
    pn, kt, rt = _proj(x0, ab_wn, ab_wt_p, ((0, hk, F32), (hk, GLA_RANK, F32)), nb, tl)
    mixed, conv_a_p, gla_p = _mix0(pn, kt, rt, ab_conv_w, w2, w2t, bg2, bg2c, ng, nb)
    x2 = prompt_tail(x0, mixed, 0, mixed, 1, ab_wo, 0)

    pn1, y_d, conv_d_p, kt1, vt1, ft1, kt16, vt16 = _proj_conf(
        x2, cd_wn, cd_wt_p, kvf + ((0, MIX_HALF, BF16), (MIX_HALF, MIX_HALF, BF16)),
        cf_conv_w, cb, cg_, cbb, nb, tl)
    lft, caug = _fox_c(ft1, bf_col)
    if in_flash:
        o_c, o_part, lf_part = _fox_flash(pn1, kt16, vt16, caug, nb, tl,
                                          (page_table, dec_small, caches, in_tails, in_flash, FLASH_PAGES))
        dec_parts[in_tails] = (o_part, lf_part)
    else:
        o_c = _fox_flash(pn1, kt16, vt16, caug, nb, tl)
    y_p = prompt_tail(x2, o_c, 0, y_d, 0, cd_wo, 1)

    y_prompt = y_p.reshape(nb, seq, d)
    k_p = jnp.transpose(kt1.reshape(nb, FOX_HEADS, FOX_HD, seq), (0, 3, 1, 2))
    v_p = jnp.transpose(vt1.reshape(nb, FOX_HEADS, FOX_HD, seq), (0, 3, 1, 2))
    lf_p = jnp.transpose(lft, (0, 2, 1))

    done = in_tails + in_flash
    if done < nd:
        dec_parts[done] = tuple(_decode(page_table, dec_small, caches, done, nd - done))
    o_col = jnp.concatenate([dec_parts[k][0] for k in sorted(dec_parts)], axis=0)
    lf_col = jnp.concatenate([dec_parts[k][1] for k in sorted(dec_parts)], axis=0)
    o_cs = o_col.reshape(nd, MIX_HALF).astype(BF16)
    y_ds, conv_d_s = _sconf(pn1_s.reshape(nd, 1, -1), state_conv_d, cf_conv_w, cb, cg_, cbb)
    y_s = tail(xs2, o_cs, 0, y_ds.reshape(nd, MIX_HALF), 0, cd_wo, 1, nd)

    y_sample = y_s.reshape(nd, 1, d)
    k_s = pn1_s[:, 3 * MIX_HALF:4 * MIX_HALF].reshape(nd, 1, FOX_HEADS, FOX_HD)
    v_s = pn1_s[:, 4 * MIX_HALF:5 * MIX_HALF].reshape(nd, 1, FOX_HEADS, FOX_HD)
    lf_s = lf_col.reshape(nd, 1, FOX_HEADS)

    return (y_prompt, y_sample, conv_a_p, conv_a_s, gla_p, gla_s, k_p, k_s, v_p, v_s, lf_p, lf_s,
            conv_d_p, conv_d_s)
```

```python
import functools

import numpy as np
import jax
import jax.numpy as jnp
from jax import lax
from jax.experimental import pallas as pl
from jax.experimental.pallas import tpu as pltpu

F32 = jnp.float32
BF16 = jnp.bfloat16

D_MODEL = 1024
MIX_HALF = D_MODEL // 2
SC_K = 3
GLA_HEADS = 4
GLA_DV = MIX_HALF // GLA_HEADS
GLA_DK = GLA_DV // 2
GLA_RANK = 16
GLA_INV_TAU = 1.0 / 16.0
GLA_CHUNK = 128
GLA_LEVELS = 7
MIX_ROWS = 512
FOX_HEADS = 8
FOX_HD = MIX_HALF // FOX_HEADS
CF_K = 31
D_FF = 2816
DEPTH = 2
DN_ALPHA = (2 * DEPTH) ** 0.25
EPS = 1e-5
R_PAD = 128
F_PAD = 16
VMEM_LIMIT = 56 * 1024 * 1024
PAGES_PER_STEP = 16
FLASH_PAGES = 16
TAIL_ROWS = 256
FFN_COL_TILE = 256
SEQ_TILE = 512
SAMPLE_SEQS_PER_STEP = 8


def _cparams(*sem):
    return pltpu.CompilerParams(dimension_semantics=sem, vmem_limit_bytes=VMEM_LIMIT)


def _log_sigmoid(x):
    return jnp.minimum(x, 0.0) - jnp.log1p(jnp.exp(-jnp.abs(x)))


def _sigmoid(x):
    return 1.0 / (1.0 + jnp.exp(-x))


def _silu(x):
    return x * _sigmoid(x)


def _layernorm(y, g, b):
    mu = jnp.mean(y, axis=-1, keepdims=True)
    d = y - mu
    var = jnp.mean(d * d, axis=-1, keepdims=True)
    return d * lax.rsqrt(var + EPS) * g + b


def _dot(a, b):
    return jnp.dot(a, b, preferred_element_type=F32)


def _dot_nt(a, b):
    return lax.dot_general(a, b, (((1,), (1,)), ((), ())), preferred_element_type=F32)


def _proj_kernel(t_outs, as_columns, x_ref, wn_ref, wt_ref, pn_ref, *pt_refs):
    xb = x_ref[...].astype(BF16)
    pn_ref[...] = _dot(xb, wn_ref[...])
    pt = _dot_nt(wt_ref[...], xb)
    for ref, (r0, n, dt) in zip(pt_refs, t_outs):
        if as_columns:
            for row in range(x_ref.shape[0]):
                ref[row] = pt[r0:r0 + n, row:row + 1].astype(dt)
        else:
            ref[0] = pt[r0:r0 + n].astype(dt)


def _proj(x2d, wn, wt, t_outs, nb, tl, as_columns=False):
    m, k = x2d.shape
    l = m // nb
    nl = l // tl
    nn = wn.shape[1]
    nt = wt.shape[0]
    out_shape = [jax.ShapeDtypeStruct((m, nn), F32)]
    out_specs = [pl.BlockSpec((tl, nn), lambda b, i: (b * nl + i, 0))]
    if as_columns:
        assert nb == 1 and nl == 1
        out_shape += [jax.ShapeDtypeStruct((m, n, 1), dt) for _, n, dt in t_outs]
        out_specs += [pl.BlockSpec((m, n, 1), lambda b, i: (0, 0, 0)) for _, n, _ in t_outs]
    else:
        out_shape += [jax.ShapeDtypeStruct((nb, n, l), dt) for _, n, dt in t_outs]
        out_specs += [pl.BlockSpec((1, n, tl), lambda b, i: (b, 0, i)) for _, n, _ in t_outs]
    return pl.pallas_call(
        functools.partial(_proj_kernel, t_outs, as_columns),
        out_shape=out_shape,
        grid=(nb, nl),
        in_specs=[pl.BlockSpec((tl, k), lambda b, i: (b * nl + i, 0)),
                  pl.BlockSpec((k, nn), lambda b, i: (0, 0)),
                  pl.BlockSpec((nt, k), lambda b, i: (0, 0))],
        out_specs=out_specs,
        compiler_params=_cparams("parallel", "arbitrary"),
        name="proj",
    )(x2d, wn, wt)


LANES = 128


def _lane_sum_mxu_wide(x):
    ones = jnp.ones((x.shape[1], LANES), BF16)
    hi, lo = _split_hi_lo(x)
    return _dot(hi, ones) + _dot(lo, ones)


def _lane_sum_mxu(x):
    return _lane_sum_mxu_wide(x)[:, 0:1]


def _per_head_rows(a):
    return jnp.concatenate([jnp.broadcast_to(a[h:h + 1, :], (FOX_HD, 1)) for h in range(FOX_HEADS)], axis=0)


def _decode_init(step, qc_ref, kc_ref, vc_ref, fc_ref, bf_ref, k_refs, v_refs, lf_refs,
                 o_ref, lfo_ref, m_ref, l_ref, r_ref, acc_ref):
    @pl.when(step == 0)
    def _():
        qc = qc_ref[0] * (FOX_HD ** -0.5)
        lf_new = _log_sigmoid(fc_ref[0] + bf_ref[...])
        lfo_ref[0] = lf_new
        r_ref[...] = lf_new
        l_ref[...] = jnp.ones_like(l_ref)
        head_col = lax.broadcasted_iota(jnp.int32, (FOX_HEADS, 1), 0)
        s_self = jnp.zeros((FOX_HEADS, 1), F32)
        for h in range(FOX_HEADS):
            hs = slice(h * FOX_HD, (h + 1) * FOX_HD)
            s_h = jnp.sum(qc[hs, :] * kc_ref[0, hs, :], axis=0, keepdims=True)
            s_self = jnp.where(head_col == h, s_h, s_self)
        m_ref[...] = s_self
        acc_ref[...] = vc_ref[0]


def _decode_logits(qc_ref, kc_ref, vc_ref, fc_ref, bf_ref, k_refs, v_refs, lf_refs,
                   o_ref, lfo_ref, m_ref, l_ref, r_ref, acc_ref):
    rows = k_refs[0].shape[-1]
    qc = qc_ref[0] * (FOX_HD ** -0.5)
    lane = lax.broadcasted_iota(jnp.int32, (FOX_HEADS, rows), 1)
    head_row = lax.broadcasted_iota(jnp.int32, (FOX_HEADS, rows), 0)
    q_wide = [jnp.broadcast_to(qc[h * FOX_HD:(h + 1) * FOX_HD, :], (FOX_HD, rows)) for h in range(FOX_HEADS)]
    r_run = r_ref[...]
    s_pages = []
    for k_ref, lf_ref in zip(k_refs, lf_refs):
        lf = lf_ref[...]
        pre = lf
        sh = 1
        while sh < rows:
            pre = pre + jnp.where(lane >= sh, pltpu.roll(pre, sh, 1), 0.0)
            sh *= 2
        tot = pre[:, rows - 1:rows]
        bias = r_run + (tot - pre)
        r_run = r_run + tot
        s = jnp.zeros((FOX_HEADS, rows), F32)
        for h in range(FOX_HEADS):
            s_h = jnp.sum(q_wide[h] * k_ref[h], axis=0, keepdims=True)
            s = jnp.where(head_row == h, s_h, s)
        s_pages.append(s + bias)
    r_ref[...] = r_run

    m_old = m_ref[...]
    m_new = m_old
    for s in s_pages:
        m_new = jnp.maximum(m_new, jnp.max(s, axis=-1, keepdims=True))
    alpha = jnp.exp(m_old - m_new)
    m_ref[...] = m_new
    p_pages = [jnp.exp(s - m_new) for s in s_pages]
    p_sum = p_pages[0]
    for p in p_pages[1:]:
        p_sum = p_sum + p
    l_ref[...] = alpha * l_ref[...] + _lane_sum_mxu(p_sum)
    return alpha, p_pages


def _decode_values(alpha, p_pages, v_refs, acc_ref):
    parts = []
    for h in range(FOX_HEADS):
        acc = p_pages[0][h:h + 1, :] * v_refs[0][h]
        for p, v_ref in zip(p_pages[1:], v_refs[1:]):
            acc = acc + p[h:h + 1, :] * v_ref[h]
        parts.append(acc)
    acc_ref[...] = _per_head_rows(alpha) * acc_ref[...] + _lane_sum_mxu(jnp.concatenate(parts, axis=0))


def _decode_finish(step, n_steps, o_ref, l_ref, acc_ref):
    @pl.when(step == n_steps - 1)
    def _():
        o_ref[0] = acc_ref[...] * _per_head_rows(1.0 / l_ref[...])


N_DEC_IN = 8
N_DEC_SCRATCH = 8


def _decode_specs(arrs, seq_of, rows, g):
    per = lambda a: pl.BlockSpec((1,) + a.shape[1:], lambda *i: (seq_of(*i[:-1]),) + (0,) * (a.ndim - 1))
    qc, kc, vc, fc, bf_col = arrs
    in_specs = [per(qc), per(kc), per(vc), per(fc), pl.BlockSpec(bf_col.shape, lambda *i: (0, 0))]
    in_specs += [pl.BlockSpec(memory_space=pl.ANY)] * 3
    scratch = [pltpu.VMEM((FOX_HEADS, 1), F32), pltpu.VMEM((FOX_HEADS, 1), F32),
               pltpu.VMEM((FOX_HEADS, 1), F32), pltpu.VMEM((MIX_HALF, 1), F32),
               pltpu.VMEM((2, g, FOX_HEADS, FOX_HD, rows), F32), pltpu.VMEM((2, g, FOX_HEADS, FOX_HD, rows), F32),
               pltpu.VMEM((2, g, FOX_HEADS, rows), F32), pltpu.SemaphoreType.DMA((2, 3))]
    return in_specs, scratch


def _decode_begin(pt_ref, ins, outs, scratch, u, n_steps, seq_of, sps, g):
    m_ref, l_ref, r_ref, acc_ref, kbuf, vbuf, lfbuf, sems = scratch
    ck, cv, clf = ins[5:]
    n_pages = pt_ref.shape[1]

    def copies(step, slot, pages_known=True):
        out = []
        for pg in range(g):
            page = pt_ref[seq_of(step), n_pages - 1 - ((step % sps) * g + pg)] if pages_known else 0
            out.append(pltpu.make_async_copy(ck.at[page], kbuf.at[slot, pg], sems.at[slot, 0]))
            out.append(pltpu.make_async_copy(cv.at[page], vbuf.at[slot, pg], sems.at[slot, 1]))
            out.append(pltpu.make_async_copy(clf.at[page], lfbuf.at[slot, pg], sems.at[slot, 2]))
        return out

    slot = u % 2

    @pl.when(u == 0)
    def _():
        for c in copies(u, slot):
            c.start()

    for c in copies(jnp.minimum(u + 1, n_steps - 1), 1 - slot):
        c.start()
    for c in copies(u, slot, pages_known=False):
        c.wait()
    views = tuple([buf.at[slot, pg] for pg in range(g)] for buf in (kbuf, vbuf, lfbuf))
    dec = (*ins[:5], *views, *outs, m_ref, l_ref, r_ref, acc_ref)

    def drain():
        @pl.when(u == n_steps - 1)
        def _():
            for c in copies(u, 1 - slot, pages_known=False):
                c.wait()

    return dec, drain


def _decode_outs(n_seq, seq_of):
    shapes = [jax.ShapeDtypeStruct((n_seq, MIX_HALF, 1), F32), jax.ShapeDtypeStruct((n_seq, FOX_HEADS, 1), F32)]
    specs = [pl.BlockSpec((1, MIX_HALF, 1), lambda *i: (seq_of(*i[:-1]), 0, 0)),
             pl.BlockSpec((1, FOX_HEADS, 1), lambda *i: (seq_of(*i[:-1]), 0, 0))]
    return shapes, specs


def _tail_kernel(seq0, steps_per_seq, *refs):
    n_dec = steps_per_seq
    if n_dec:
        pt_ref, refs = refs[0], refs[1:]
    h1_ref, h2_ref, wo_ref, x_ref, g1_ref, b1_ref, wg_ref, wu_ref, wd_ref, g2_ref, b2_ref = refs[:11]
    h1n_ref, h2n_ref, xn_ref = refs[11:14]
    refs = refs[14:]
    if n_dec:
        dec_in, refs = refs[:N_DEC_IN], refs[N_DEC_IN:]
    o_ref, x1_ref = refs[0], refs[-1]
    half = h1_ref.shape[1]

    def mixed_in(h1, h2, x):
        mix = _dot(h1[...], wo_ref[0:half, :]) + _dot(h2[...], wo_ref[half:2 * half, :])
        return _layernorm(DN_ALPHA * x[...] + mix, g1_ref[...], b1_ref[...])

    @pl.when(pl.program_id(0) == 0)
    def _():
        x1_ref[...] = mixed_in(h1_ref, h2_ref, x_ref)

    if n_dec:
        u = pl.program_id(0)
        step = u % steps_per_seq
        dec, drain = _decode_begin(pt_ref, dec_in, refs[1:3], refs[3:-1], u, pl.num_programs(0),
                                   lambda s: seq0 + s // steps_per_seq, steps_per_seq, PAGES_PER_STEP)
        _decode_init(step, *dec)
    x1 = x1_ref[...]
    xb = x1.astype(BF16)
    dff = wg_ref.shape[1]
    cut = (dff // (2 * FFN_COL_TILE) + 1) * FFN_COL_TILE if dff > 2 * FFN_COL_TILE else dff
    parts = [(0, cut), (cut, dff)] if cut < dff else [(0, dff)]
    gate_up = [(_dot(xb, wg_ref[:, a:b]), _dot(xb, wu_ref[:, a:b])) for a, b in parts]
    if n_dec:
        alpha, p_pages = _decode_logits(*dec)
    x1_next = mixed_in(h1n_ref, h2n_ref, xn_ref)
    if n_dec:
        _decode_values(alpha, p_pages, dec[6], dec[13])
    down = None
    for (a, b), (gate, up) in zip(parts, gate_up):
        d = _dot((_silu(gate) * up).astype(BF16), wd_ref[a:b, :])
        down = d if down is None else down + d
    x1_ref[...] = x1_next
    o_ref[...] = _layernorm(DN_ALPHA * x1 + down, g2_ref[...], b2_ref[...])
    if n_dec:
        _decode_finish(step, steps_per_seq, dec[8], dec[11], dec[13])
        drain()


def _tail(h1, h1_col, h2, h2_col, wo, x2d, g1, b1, w_in, wd, layer, g2, b2, tm, dec=None):
    m, d = x2d.shape
    half = d // 2
    dff = wd.shape[1]
    nsteps = m // tm
    once = lambda shape, imap: pl.BlockSpec(shape, imap, pipeline_mode=pl.Buffered(1))
    in_specs = [pl.BlockSpec((tm, half), lambda i, *_: (i, h1_col)),
                pl.BlockSpec((tm, half), lambda i, *_: (i, h2_col)),
                once((d, d), lambda i, *_: (0, 0)),
                pl.BlockSpec((tm, d), lambda i, *_: (i, 0)),
                pl.BlockSpec((1, d), lambda i, *_: (0, 0)),
                pl.BlockSpec((1, d), lambda i, *_: (0, 0)),
                once((None, d, dff), lambda i, *_: (layer, 0, 0)),
                once((None, d, dff), lambda i, *_: (layer, 0, 1)),
                once((None, dff, d), lambda i, *_: (layer, 0, 0)),
                pl.BlockSpec((1, d), lambda i, *_: (0, 0)),
                pl.BlockSpec((1, d), lambda i, *_: (0, 0))]
    nxt = lambda i: jnp.minimum(i + 1, nsteps - 1)
    in_specs += [pl.BlockSpec((tm, half), lambda i, *_: (nxt(i), h1_col)),
                 pl.BlockSpec((tm, half), lambda i, *_: (nxt(i), h2_col)),
                 pl.BlockSpec((tm, d), lambda i, *_: (nxt(i), 0))]
    out_shape = [jax.ShapeDtypeStruct((m, d), F32)]
    out_specs = [pl.BlockSpec((tm, d), lambda i, *_: (i, 0))]
    x1_scratch = [pltpu.VMEM((tm, d), F32)]
    args = [h1, h2, wo, x2d, g1, b1, w_in, w_in, wd, g2, b2, h1, h2, x2d]
    if dec is None:
        return pl.pallas_call(
            functools.partial(_tail_kernel, 0, 0), out_shape=out_shape, grid=(nsteps,), in_specs=in_specs,
            out_specs=out_specs, scratch_shapes=x1_scratch,
            compiler_params=_cparams("arbitrary"), name="tail")(*args)[0]
    page_table, small, (ck, cv, clf), seq0, n_seq = dec
    n_pages = page_table.shape[1]
    g = PAGES_PER_STEP
    sps = n_pages // g
    assert nsteps == n_seq * sps
    seq_of = lambda i: seq0 + i // sps
    dec_specs, scratch = _decode_specs(small, seq_of, ck.shape[3], g)
    d_shapes, d_specs = _decode_outs(n_seq, lambda i: i // sps)
    grid_spec = pltpu.PrefetchScalarGridSpec(
        num_scalar_prefetch=1, grid=(nsteps,), in_specs=in_specs + dec_specs, out_specs=out_specs + d_specs,
        scratch_shapes=scratch + x1_scratch)
    return pl.pallas_call(
        functools.partial(_tail_kernel, seq0, sps), out_shape=out_shape + d_shapes, grid_spec=grid_spec,
        compiler_params=_cparams("arbitrary"), name="tail_decode",
    )(page_table, *args, *small, ck, cv, clf)


def _gla_level_masks():
    c = GLA_CHUNK
    t = np.arange(c)[:, None]
    s = np.arange(c)[None, :]
    x = t ^ s
    masks = [x == 0] + [(t > s) & (x >= (1 << (l - 1))) & (x < (1 << l)) for l in range(1, GLA_LEVELS + 1)]
    m = np.stack(masks).astype(np.float32)
    return jnp.asarray(np.concatenate([m, m], axis=2))


def _gla_sum_matrices():
    c = GLA_CHUNK
    t = np.arange(c)[:, None]
    u = np.arange(c)[None, :]
    mq = [(u <= t)]
    mk = [(u > t)]
    for l in range(1, GLA_LEVELS + 1):
        blk, half = 1 << l, 1 << (l - 1)
        same = (t // blk) == (u // blk)
        mq.append(same & (t % blk >= half) & (u % blk >= half) & (u <= t))
        mk.append(same & (t % blk < half) & (u % blk < half) & (u > t))
    mk.append(np.ones((c, c), bool))
    mq = np.concatenate(mq, axis=0).astype(np.float32)
    mk = np.concatenate(mk, axis=0).astype(np.float32).T
    mq2 = np.concatenate([mq, mq], axis=1)
    mk2 = np.concatenate([mk, mk], axis=0)
    return jnp.asarray(mq2, BF16), jnp.asarray(mk2, BF16)


def _split_hi_lo(x):
    hi = x.astype(BF16)
    lo = (x - hi.astype(F32)).astype(BF16)
    return hi, lo


def _mix0_kernel(bg_ref, cg_ref, hv_ref, v_ref, gt_ref, q_ref, r_ref, kt_ref, rt_ref,
                 convw_ref, w2_ref, w2t_ref, bg2_ref, bg2c_ref, ng_ref, mq_ref, mk_ref, lvl_ref,
                 mixed_ref, convst_ref, glast_ref, s_ref, ext_ref):
    c = GLA_CHUNK
    rows = bg_ref.shape[0]
    i = pl.program_id(1)

    @pl.when(i == 0)
    def _():
        s_ref[...] = jnp.zeros_like(s_ref)
        ext_ref[0:8, :] = jnp.zeros((8, MIX_HALF), F32)

    u = cg_ref[...] * hv_ref[...]
    ext_ref[8:8 + rows, :] = u
    w = convw_ref[...]
    y = w[2:3, :] * u + w[1:2, :] * ext_ref[7:7 + rows, :] + w[0:1, :] * ext_ref[6:6 + rows, :]
    mixed_ref[:, 0:MIX_HALF] = (bg_ref[...] * y).astype(BF16)
    ext_ref[0:8, :] = u[rows - 8:rows, :]
    convst_ref[0] = u[rows - (SC_K - 1):rows, :]

    pair = 2 * GLA_DK
    own_block = ((lax.broadcasted_iota(jnp.int32, (pair, 2 * c), 0) < GLA_DK)
                 == (lax.broadcasted_iota(jnp.int32, (pair, 2 * c), 1) < c))

    def pair_dot(qp, ktp):
        kbd = jnp.where(own_block, jnp.concatenate([ktp, ktp], axis=1), 0.0).astype(BF16)
        return _dot(qp.astype(BF16), kbd)

    chunks = []
    for n in range(rows // c):
        rs = slice(n * c, (n + 1) * c)
        z = _dot(r_ref[rs, :].astype(BF16), w2_ref[...]) + bg2_ref[...]
        la = _log_sigmoid(z) * GLA_INV_TAU
        zt = _dot(w2t_ref[...], rt_ref[0, :, rs].astype(BF16)) + bg2c_ref[...]
        lat = _log_sigmoid(zt) * GLA_INV_TAU
        hi, lo = _split_hi_lo(la)
        eq = _dot(mq_ref[...], jnp.concatenate([hi, lo], axis=0))
        hit, lot = _split_hi_lo(lat)
        ek = _dot(jnp.concatenate([hit, lot], axis=1), mk_ref[...])
        chunks.append((rs, q_ref[rs, :] * (GLA_DK ** -0.5), kt_ref[0, :, rs], eq, ek))

    all_scores = []
    for rs, q, kt, eq, ek in chunks:
        scores = []
        for hp in range(GLA_HEADS // 2):
            ps = slice(hp * pair, (hp + 1) * pair)
            qp = q[:, ps]
            ktp = kt[ps, :]
            a = lvl_ref[0] * pair_dot(qp, ktp)
            for l in range(1, GLA_LEVELS + 1):
                ql = qp * jnp.exp(eq[l * c:(l + 1) * c, ps])
                kl = ktp * jnp.exp(ek[ps, l * c:(l + 1) * c])
                a = a + lvl_ref[l] * pair_dot(ql, kl)
            a = a.astype(BF16)
            scores += [a[:, 0:c], a[:, c:2 * c]]
        all_scores.append(scores)

    for (rs, q, kt, eq, ek), scores in zip(chunks, all_scores):
        for h in range(GLA_HEADS):
            ks = slice(h * GLA_DK, (h + 1) * GLA_DK)
            vs = slice(h * GLA_DV, (h + 1) * GLA_DV)
            qh = q[:, ks]
            kth = kt[ks, :]
            vh = v_ref[rs, vs].astype(BF16)
            s_old = s_ref[h]
            q_read = (qh * jnp.exp(eq[0:c, ks])).astype(BF16)
            o = _dot(scores[h], vh) + _dot(q_read, s_old.astype(BF16))
            k_write = (kth * jnp.exp(ek[ks, 0:c])).astype(BF16)
            a_chunk = jnp.exp(ek[ks, (GLA_LEVELS + 1) * c:(GLA_LEVELS + 2) * c])
            s_ref[h] = a_chunk * s_old + _dot(k_write, vh)
            on = o * lax.rsqrt(jnp.mean(o * o, axis=-1, keepdims=True) + EPS) * ng_ref[...]
            mixed_ref[rs, MIX_HALF + h * GLA_DV:MIX_HALF + (h + 1) * GLA_DV] = (
                on * _silu(gt_ref[rs, vs])).astype(BF16)
    glast_ref[0] = s_ref[...]


def _mix0(pn, kt, rt, conv_w, w2, w2t, bg2, bg2c, ng, nb):
    m = pn.shape[0]
    l = m // nb
    c = min(l, MIX_ROWS)
    assert c % GLA_CHUNK == 0 and l % c == 0
    nl = l // c
    mq, mk = _gla_sum_matrices()
    lvl = _gla_level_masks()
    row = lambda b, i: b * nl + i
    full = lambda a: pl.BlockSpec(a.shape, lambda b, i: (0,) * a.ndim)
    wide = lambda col: pl.BlockSpec((c, MIX_HALF), lambda b, i: (row(b, i), col))
    q_col = 5 * MIX_HALF // (GLA_HEADS * GLA_DK)
    r_col = (5 * MIX_HALF + GLA_HEADS * GLA_DK) // R_PAD
    return pl.pallas_call(
        _mix0_kernel,
        out_shape=[jax.ShapeDtypeStruct((m, D_MODEL), BF16),
                   jax.ShapeDtypeStruct((nb, SC_K - 1, MIX_HALF), F32),
                   jax.ShapeDtypeStruct((nb, GLA_HEADS, GLA_DK, GLA_DV), F32)],
        grid=(nb, nl),
        in_specs=[wide(0), wide(1), wide(2), wide(3), wide(4),
                  pl.BlockSpec((c, GLA_HEADS * GLA_DK), lambda b, i: (row(b, i), q_col)),
                  pl.BlockSpec((c, R_PAD), lambda b, i: (row(b, i), r_col)),
                  pl.BlockSpec((1, GLA_HEADS * GLA_DK, c), lambda b, i: (b, 0, i)),
                  pl.BlockSpec((1, GLA_RANK, c), lambda b, i: (b, 0, i)),
                  full(conv_w), full(w2), full(w2t), full(bg2), full(bg2c), full(ng), full(mq), full(mk),
                  full(lvl)],
        out_specs=[pl.BlockSpec((c, D_MODEL), lambda b, i: (row(b, i), 0)),
                   pl.BlockSpec((1, SC_K - 1, MIX_HALF), lambda b, i: (b, 0, 0)),
                   pl.BlockSpec((1, GLA_HEADS, GLA_DK, GLA_DV), lambda b, i: (b, 0, 0, 0))],
        scratch_shapes=[pltpu.VMEM((GLA_HEADS, GLA_DK, GLA_DV), F32), pltpu.VMEM((c + 8, MIX_HALF), F32)],
        compiler_params=_cparams("parallel", "arbitrary"),
        name="mix0",
    )(pn, pn, pn, pn, pn, pn, pn, kt, rt, conv_w, w2, w2t, bg2, bg2c, ng, mq, mk, lvl)


C_ROWS = 16
LOG2E = 1.4426950408889634


def _fox_c_kernel(ft_ref, bf_ref, lf_ref, c_ref):
    lf = _log_sigmoid(ft_ref[0] + bf_ref[...])
    lf_ref[0] = lf
    n = lf.shape[1]
    lane = lax.broadcasted_iota(jnp.int32, lf.shape, 1)
    suf = lf
    s = 1
    while s < n:
        suf = suf + jnp.where(lane + s < n, pltpu.roll(suf, n - s, 1), 0.0)
        s *= 2
    bias = (suf - lf) * LOG2E
    hi = bias.astype(BF16).astype(F32)
    r1 = bias - hi
    mid = r1.astype(BF16).astype(F32)
    lo = (r1 - mid).astype(BF16).astype(F32)
    row = lax.broadcasted_iota(jnp.int32, (C_ROWS, n), 0)
    for h in range(FOX_HEADS):
        hs = slice(h, h + 1)
        parts = jnp.where(row == 0, hi[hs], jnp.where(row == 1, mid[hs], jnp.where(row == 2, lo[hs], 0.0)))
        c_ref[0, h] = parts.astype(BF16)


def _fox_c(ft, bf_col):
    nb, _, l = ft.shape
    spec = pl.BlockSpec((1, FOX_HEADS, l), lambda b: (b, 0, 0))
    return pl.pallas_call(
        _fox_c_kernel,
        out_shape=[jax.ShapeDtypeStruct((nb, FOX_HEADS, l), F32),
                   jax.ShapeDtypeStruct((nb, FOX_HEADS, C_ROWS, l), BF16)],
        grid=(nb,),
        in_specs=[spec, pl.BlockSpec((FOX_HEADS, 1), lambda b: (0, 0))],
        out_specs=[spec, pl.BlockSpec((1, FOX_HEADS, C_ROWS, l), lambda b: (b, 0, 0, 0))],
        compiler_params=_cparams("parallel"),
        name="fox_c",
    )(ft, bf_col)


AUG = 2 * FOX_HD
FLASH_AHEAD = 8


def _fox_flash_kernel(dec_pages, steps_per_seq, seq0, n_seq, *refs):
    sched_ref = refs[0]
    if dec_pages:
        pt_ref = refs[1]
        refs = refs[2:]
    else:
        refs = refs[1:]
    q_ref, kt_ref, vt_ref, c_ref = refs[:4]
    refs = refs[4:]
    t = pl.program_id(0)
    if dec_pages:
        dec_in, refs = refs[:N_DEC_IN], refs[N_DEC_IN:]
        o_ref, do_ref, dlf_ref, qa_ref, m_ref, acc_ref = refs[:6]
        seq_of = lambda s: seq0 + jnp.minimum(s // steps_per_seq, n_seq - 1)
        dec, drain = _decode_begin(pt_ref, dec_in, (do_ref, dlf_ref), refs[6:], t, pl.num_programs(0),
                                   seq_of, steps_per_seq, dec_pages)
        dacc_ref = dec[13]
    else:
        o_ref, qa_ref, m_ref, acc_ref = refs
    qi = sched_ref[1, t]
    kj = sched_ref[2, t]
    tq = q_ref.shape[0]
    tk = kt_ref.shape[2]
    lane = lax.broadcasted_iota(jnp.int32, (tq, AUG), 1)
    if dec_pages:
        dstep = t % steps_per_seq
        _decode_init(dstep, *dec)

    @pl.when(kj == 0)
    def _():
        ones3 = jnp.where(lane < FOX_HD + 3, 1.0, 0.0)
        for p in range(FOX_HEADS // 2):
            qp = q_ref[:, p * AUG:(p + 1) * AUG] * (FOX_HD ** -0.5 * LOG2E)
            qa_ref[2 * p] = jnp.where(lane < FOX_HD, qp, ones3).astype(BF16)
            qa_ref[2 * p + 1] = jnp.where(lane < FOX_HD, pltpu.roll(qp, FOX_HD, 1), ones3).astype(BF16)
        m_ref[...] = jnp.full_like(m_ref, -jnp.inf)
        acc_ref[...] = jnp.zeros_like(acc_ref)

    def block(masked):
        if masked:
            visible = (lax.broadcasted_iota(jnp.int32, (tq, tk), 1) <= lax.broadcasted_iota(jnp.int32, (tq, tk), 0))
        pad_k = jnp.zeros((AUG - FOX_HD - C_ROWS, tk), BF16)
        ones_row = jnp.where(lax.broadcasted_iota(jnp.int32, (AUG - FOX_HD, tk), 0) == 0, 1.0, 0.0).astype(BF16)

        def logits(h):
            hs = slice(h * FOX_HD, (h + 1) * FOX_HD)
            ka = jnp.concatenate([kt_ref[0, hs, :], c_ref[0, h], pad_k], axis=0)
            return _dot(qa_ref[h], ka)

        s_ahead = [logits(h) for h in range(FLASH_AHEAD)]
        if dec_pages:
            alpha_d, p_d = _decode_logits(*dec)
        for h in range(FOX_HEADS):
            if dec_pages and h == FOX_HEADS // 2:
                _decode_values(alpha_d, p_d, dec[6], dacc_ref)
            if h + FLASH_AHEAD < FOX_HEADS:
                s_ahead.append(logits(h + FLASH_AHEAD))
            s = s_ahead.pop(0)
            hs = slice(h * FOX_HD, (h + 1) * FOX_HD)
            va = jnp.concatenate([vt_ref[0, hs, :], ones_row], axis=0)
            if masked:
                s = jnp.where(visible, s, -jnp.inf)
            m_old = m_ref[h]
            m_new = jnp.maximum(m_old, jnp.max(s, axis=-1, keepdims=True))
            pr = jnp.exp2(s - jnp.concatenate([m_new] * (tk // AUG), axis=1)).astype(BF16)
            acc_ref[h] = jnp.exp2(m_old - m_new) * acc_ref[h] + _dot_nt(pr, va)
            m_ref[h] = m_new

    @pl.when(kj < qi)
    def _():
        block(False)

    @pl.when(kj == qi)
    def _():
        block(True)
        for p in range(FOX_HEADS // 2):
            a0 = acc_ref[2 * p]
            a1 = acc_ref[2 * p + 1]
            o0 = a0 * (1.0 / a0[:, FOX_HD:FOX_HD + 1])
            o1 = a1 * (1.0 / a1[:, FOX_HD:FOX_HD + 1])
            o_ref[:, p * AUG:(p + 1) * AUG] = jnp.where(lane < FOX_HD, o0, pltpu.roll(o1, FOX_HD, 1)).astype(BF16)

    if dec_pages:
        _decode_finish(dstep, steps_per_seq, do_ref, dec[11], dacc_ref)
        drain()


def _fox_flash(pn, kt, vt, caug, nb, tq, dec=None):
    m = pn.shape[0]
    l = m // nb
    nq = l // tq
    sched = jnp.asarray(np.array([(b, i, j) for b in range(nb) for i in range(nq) for j in range(i + 1)],
                                 np.int32).T)
    steps = sched.shape[1]
    kv_spec = pl.BlockSpec((1, MIX_HALF, tq), lambda t, sc, *_: (sc[0, t], 0, sc[2, t]))
    q_map = lambda t, sc, *_: (sc[0, t] * nq + sc[1, t], 0)
    in_specs = [pl.BlockSpec((tq, MIX_HALF), q_map), kv_spec, kv_spec,
                pl.BlockSpec((1, FOX_HEADS, C_ROWS, tq), lambda t, sc, *_: (sc[0, t], 0, 0, sc[2, t]))]
    out_shape = [jax.ShapeDtypeStruct((m, MIX_HALF), BF16)]
    out_specs = [pl.BlockSpec((tq, MIX_HALF), q_map)]
    scratch = [pltpu.VMEM((FOX_HEADS, tq, AUG), BF16), pltpu.VMEM((FOX_HEADS, tq, AUG), F32),
               pltpu.VMEM((FOX_HEADS, tq, AUG), F32)]
    if dec is None:
        grid_spec = pltpu.PrefetchScalarGridSpec(
            num_scalar_prefetch=1, grid=(steps,), in_specs=in_specs, out_specs=out_specs, scratch_shapes=scratch)
        return pl.pallas_call(
            functools.partial(_fox_flash_kernel, 0, 0, 0, 0), out_shape=out_shape, grid_spec=grid_spec,
            compiler_params=_cparams("arbitrary"), name="fox_flash")(sched, pn, kt, vt, caug)[0]
    page_table, small, (ck, cv, clf), seq0, n_seq, g = dec
    n_pages = page_table.shape[1]
    sps = n_pages // g
    slots = steps // sps
    assert steps % sps == 0 and slots >= n_seq
    seq_of = lambda t, *_: seq0 + jnp.minimum(t // sps, n_seq - 1)
    dec_specs, dec_scratch = _decode_specs(small, seq_of, ck.shape[3], g)
    d_shapes, d_specs = _decode_outs(slots, lambda t, *_: t // sps)
    grid_spec = pltpu.PrefetchScalarGridSpec(
        num_scalar_prefetch=2, grid=(steps,), in_specs=in_specs + dec_specs, out_specs=out_specs + d_specs,
        scratch_shapes=scratch + dec_scratch)
    o_c, o_part, lf_part = pl.pallas_call(
        functools.partial(_fox_flash_kernel, g, sps, seq0, n_seq), out_shape=out_shape + d_shapes,
        grid_spec=grid_spec, compiler_params=_cparams("arbitrary"), name="fox_flash_decode",
    )(sched, page_table, pn, kt, vt, caug, *small, ck, cv, clf)
    return o_c, o_part[:n_seq], lf_part[:n_seq]


CF_HALO = 32
SUBLANES = 8


CONF_ROWS = 256


def _conf_rows(a, gate, w_ref, cb_ref, g_ref, b_ref, ext_ref):
    t = a.shape[0]
    u = a * _sigmoid(gate)
    ext_ref[CF_HALO:CF_HALO + t, :] = u
    off = CF_HALO - (CF_K - 1)
    acc = jnp.zeros((t, MIX_HALF), F32) + cb_ref[...]
    for rho in range(SUBLANES):
        taps = [j for j in range(CF_K) if (off + j) % SUBLANES == rho]
        rows = t if rho == 0 else t + SUBLANES
        part = jnp.zeros((rows, MIX_HALF), F32)
        for j in taps:
            a0 = off + j - rho
            part = part + w_ref[j:j + 1, :] * ext_ref[a0:a0 + rows, :]
        acc = acc + part[rho:rho + t, :]
    ext_ref[0:CF_HALO, :] = ext_ref[t:t + CF_HALO, :]
    wide = lambda r: jnp.concatenate([r] * (MIX_HALF // LANES), axis=1)
    d = acc - wide(_lane_sum_mxu_wide(acc) * (1.0 / MIX_HALF))
    inv = lax.rsqrt(_lane_sum_mxu_wide(d * d) * (1.0 / MIX_HALF) + EPS)
    return _silu(d * wide(inv) * g_ref[...] + b_ref[...]).astype(BF16)


def _proj_conf_kernel(t_outs, x_ref, wn_ref, wt_ref, w_ref, cb_ref, g_ref, b_ref, q_ref, y_ref, st_ref, *rest):
    pt_refs, ext_ref = rest[:-1], rest[-1]
    tl = x_ref.shape[0]
    sub = min(tl, CONF_ROWS)

    @pl.when(pl.program_id(1) == 0)
    def _():
        ext_ref[0:CF_HALO, :] = jnp.zeros((CF_HALO, MIX_HALF), F32)

    xb = x_ref[...].astype(BF16)
    n_groups = tl // sub
    group = lambda n: _dot(xb[n * sub:(n + 1) * sub], wn_ref[...])
    pn_next = group(0)
    pt = None
    for n in range(n_groups):
        pn = pn_next
        if n + 1 < n_groups:
            pn_next = group(n + 1)
        else:
            pt = _dot_nt(wt_ref[...], xb)
        rs = slice(n * sub, (n + 1) * sub)
        q_ref[rs, :] = pn[:, 0:MIX_HALF]
        y_ref[rs, :] = _conf_rows(pn[:, MIX_HALF:2 * MIX_HALF], pn[:, 2 * MIX_HALF:3 * MIX_HALF],
                                  w_ref, cb_ref, g_ref, b_ref, ext_ref)
    st_ref[0] = ext_ref[CF_HALO - (CF_K - 1):CF_HALO, :]
    for ref, (r0, n, dt) in zip(pt_refs, t_outs):
        ref[0] = pt[r0:r0 + n].astype(dt)


def _proj_conf(x2d, wn, wt, t_outs, w, cb, g, b, nb, tl):
    m, k = x2d.shape
    l = m // nb
    nl = l // tl
    full = lambda a: pl.BlockSpec(a.shape, lambda bb, i: (0,) * a.ndim)
    rows = lambda bb, i: (bb * nl + i, 0)
    out_shape = [jax.ShapeDtypeStruct((m, MIX_HALF), F32), jax.ShapeDtypeStruct((m, MIX_HALF), BF16),
                 jax.ShapeDtypeStruct((nb, CF_K - 1, MIX_HALF), F32)]
    out_shape += [jax.ShapeDtypeStruct((nb, n, l), dt) for _, n, dt in t_outs]
    out_specs = [pl.BlockSpec((tl, MIX_HALF), rows), pl.BlockSpec((tl, MIX_HALF), rows),
                 pl.BlockSpec((1, CF_K - 1, MIX_HALF), lambda bb, i: (bb, 0, 0))]
    out_specs += [pl.BlockSpec((1, n, tl), lambda bb, i: (bb, 0, i)) for _, n, _ in t_outs]
    return pl.pallas_call(
        functools.partial(_proj_conf_kernel, t_outs),
        out_shape=out_shape,
        grid=(nb, nl),
        in_specs=[pl.BlockSpec((tl, k), rows), full(wn), full(wt), full(w), full(cb), full(g), full(b)],
        out_specs=out_specs,
        scratch_shapes=[pltpu.VMEM((min(tl, CONF_ROWS) + CF_HALO, MIX_HALF), F32)],
        compiler_params=_cparams("parallel", "arbitrary"),
        name="proj_conf",
    )(x2d, wn, wt, w, cb, g, b)


def _smix0_kernel(pn_ref, kc_ref, qc_ref, rc_ref, prev_ref, s_ref, convw_ref, w2t_ref, bg2c_ref, ng_ref,
                  mixed_ref, convst_ref, sout_ref):
    w = convw_ref[...]
    w2t = w2t_ref[...]
    for i in range(pn_ref.shape[0]):
        pn = pn_ref[i]
        bg, cg, hv = pn[:, 0:MIX_HALF], pn[:, MIX_HALF:2 * MIX_HALF], pn[:, 2 * MIX_HALF:3 * MIX_HALF]
        v, gt = pn[:, 3 * MIX_HALF:4 * MIX_HALF], pn[:, 4 * MIX_HALF:5 * MIX_HALF]
        u = cg * hv
        prev = prev_ref[i]
        y = w[0:1, :] * prev[0:1, :] + w[1:2, :] * prev[1:2, :] + w[2:3, :] * u
        mixed_ref[i, :, 0:MIX_HALF] = (bg * y).astype(BF16)
        convst_ref[i, 0:1, :] = prev[1:2, :]
        convst_ref[i, 1:2, :] = u

        zc = _dot(w2t, rc_ref[i].astype(BF16)) + bg2c_ref[...]
        ac = jnp.exp(_log_sigmoid(zc) * GLA_INV_TAU)
        qc = qc_ref[i] * (GLA_DK ** -0.5)
        kc = kc_ref[i]
        for h in range(GLA_HEADS):
            ks = slice(h * GLA_DK, (h + 1) * GLA_DK)
            vs = slice(h * GLA_DV, (h + 1) * GLA_DV)
            s_new = ac[ks, :] * s_ref[i, h] + kc[ks, :] * v[:, vs]
            sout_ref[i, h] = s_new
            o = jnp.sum(qc[ks, :] * s_new, axis=0, keepdims=True)
            on = o * lax.rsqrt(jnp.mean(o * o, axis=-1, keepdims=True) + EPS) * ng_ref[...]
            mixed_ref[i, :, MIX_HALF + h * GLA_DV:MIX_HALF + (h + 1) * GLA_DV] = (
                on * _silu(gt[:, vs])).astype(BF16)


def _smix0(pn3, kc, qc, rc, prev, s0, conv_w, w2t, bg2c, ng):
    nd, _, nn = pn3.shape
    sb = _tile_rows(nd, SAMPLE_SEQS_PER_STEP)
    per = lambda a: pl.BlockSpec((sb,) + a.shape[1:], lambda b: (b,) + (0,) * (a.ndim - 1))
    full = lambda a: pl.BlockSpec(a.shape, lambda b: (0,) * a.ndim)
    out_shape = [jax.ShapeDtypeStruct((nd, 1, D_MODEL), BF16),
                 jax.ShapeDtypeStruct((nd, SC_K - 1, MIX_HALF), F32),
                 jax.ShapeDtypeStruct(s0.shape, F32)]
    return pl.pallas_call(
        _smix0_kernel,
        out_shape=out_shape,
        grid=(nd // sb,),
        in_specs=[per(pn3), per(kc), per(qc), per(rc), per(prev), per(s0),
                  full(conv_w), full(w2t), full(bg2c), full(ng)],
        out_specs=[per(o) for o in out_shape],
        compiler_params=_cparams("parallel"),
        name="smix0",
    )(pn3, kc, qc, rc, prev, s0, conv_w, w2t, bg2c, ng)


def _sconf_kernel(pn_ref, prev_ref, w_ref, cb_ref, g_ref, b_ref, y_ref, st_ref):
    w = w_ref[...]
    for i in range(pn_ref.shape[0]):
        pn = pn_ref[i]
        a, gate = pn[:, MIX_HALF:2 * MIX_HALF], pn[:, 2 * MIX_HALF:3 * MIX_HALF]
        u = a * _sigmoid(gate)
        prev = prev_ref[i]
        acc = jnp.sum(w[0:CF_K - 1, :] * prev, axis=0, keepdims=True) + w[CF_K - 1:CF_K, :] * u + cb_ref[...]
        y_ref[i] = _silu(_layernorm(acc, g_ref[...], b_ref[...])).astype(BF16)
        st_ref[i, 0:CF_K - 2, :] = prev_ref[i, 1:CF_K - 1, :]
        st_ref[i, CF_K - 2:CF_K - 1, :] = u


def _sconf(pn3, prev, w, cb, g, b):
    nd = pn3.shape[0]
    sb = _tile_rows(nd, SAMPLE_SEQS_PER_STEP)
    per = lambda a: pl.BlockSpec((sb,) + a.shape[1:], lambda bb: (bb,) + (0,) * (a.ndim - 1))
    full = lambda a: pl.BlockSpec(a.shape, lambda bb: (0,) * a.ndim)
    out_shape = [jax.ShapeDtypeStruct((nd, 1, MIX_HALF), BF16), jax.ShapeDtypeStruct(prev.shape, F32)]
    return pl.pallas_call(
        _sconf_kernel,
        out_shape=out_shape,
        grid=(nd // sb,),
        in_specs=[per(pn3), per(prev), full(w), full(cb), full(g), full(b)],
        out_specs=[per(o) for o in out_shape],
        compiler_params=_cparams("parallel"),
        name="sconf",
    )(pn3, prev, w, cb, g, b)


def _decode_kernel(seq0, steps_per_seq, pt_ref, *refs):
    u = pl.program_id(0)
    step = u % steps_per_seq
    dec, drain = _decode_begin(pt_ref, refs[:N_DEC_IN], refs[N_DEC_IN:N_DEC_IN + 2], refs[N_DEC_IN + 2:], u,
                               pl.num_programs(0), lambda s: seq0 + s // steps_per_seq, steps_per_seq,
                               PAGES_PER_STEP)
    _decode_init(step, *dec)
    alpha, p_pages = _decode_logits(*dec)
    _decode_values(alpha, p_pages, dec[6], dec[13])
    _decode_finish(step, steps_per_seq, dec[8], dec[11], dec[13])
    drain()


def _decode(page_table, small, caches, seq0, n_seq):
    ck, cv, clf = caches
    g = PAGES_PER_STEP
    sps = page_table.shape[1] // g
    in_specs, scratch = _decode_specs(small, lambda u: seq0 + u // sps, ck.shape[3], g)
    out_shape, out_specs = _decode_outs(n_seq, lambda u: u // sps)
    grid_spec = pltpu.PrefetchScalarGridSpec(
        num_scalar_prefetch=1, grid=(n_seq * sps,), in_specs=in_specs, out_specs=out_specs,
        scratch_shapes=scratch)
    return pl.pallas_call(
        functools.partial(_decode_kernel, seq0, sps),
        out_shape=out_shape,
        grid_spec=grid_spec,
        compiler_params=_cparams("arbitrary"),
        name="fox_decode",
    )(page_table, *small, ck, cv, clf)


def _tile_rows(n, pref):
    t = min(n, pref)
    while n % t:
        t //= 2
    return t


def kernel(x_prompt, x_sample, state_conv_a, state_gla, cache_k, cache_v, cache_logf, state_conv_d, page_table,
           ab_w_in, ab_conv_w, gla_w_gate2, gla_b_gate, gla_norm_g, ab_w_out,
           cd_w_in, fox_b_f, cf_conv_w, cf_conv_b, cf_ln_g, cf_ln_b, cd_w_out,
           ffn_w_in, ffn_w_out, ln_g, ln_b):
    nb, seq, d = x_prompt.shape
    nd = x_sample.shape[0]
    hk = GLA_HEADS * GLA_DK

    o_bg, o_cg, o_hv = 0, MIX_HALF, 2 * MIX_HALF
    o_q = 3 * MIX_HALF
    o_k = o_q + hk
    o_v = o_k + hk
    o_g = o_v + MIX_HALF
    o_r = o_g + MIX_HALF
    wab = ab_w_in.astype(BF16)
    ab_wn = jnp.concatenate([wab[:, o_bg:o_q], wab[:, o_v:o_g], wab[:, o_g:o_r], wab[:, o_q:o_k],
                             jnp.pad(wab[:, o_r:o_r + GLA_RANK], ((0, 0), (0, R_PAD - GLA_RANK)))], axis=1)
    wab_t = wab.T
    ab_wt_p = jnp.concatenate([wab_t[o_k:o_v], wab_t[o_r:o_r + GLA_RANK]], axis=0)
    ab_wt_s = jnp.concatenate([ab_wt_p, wab_t[o_q:o_k]], axis=0)
    w2 = jnp.pad(gla_w_gate2.astype(BF16), ((0, R_PAD - GLA_RANK), (0, 0)))
    w2t = gla_w_gate2.T.astype(BF16)
    bg2 = gla_b_gate.reshape(1, hk)
    bg2c = gla_b_gate.reshape(hk, 1)
    ng = gla_norm_g.reshape(1, GLA_DV)
    ab_wo = ab_w_out.astype(BF16)

    wcd = cd_w_in.astype(BF16)
    c_q, c_k, c_v = 0, MIX_HALF, 2 * MIX_HALF
    c_f = 3 * MIX_HALF
    c_a = c_f + FOX_HEADS
    c_gate = c_a + MIX_HALF
    cd_wn = jnp.concatenate([wcd[:, c_q:c_k], wcd[:, c_a:c_gate], wcd[:, c_gate:c_gate + MIX_HALF]], axis=1)
    cd_wn_s = jnp.concatenate([cd_wn, wcd[:, c_k:c_f]], axis=1)
    wcd_t = wcd.T
    cd_wt_p = jnp.concatenate([wcd_t[c_k:c_v], wcd_t[c_v:c_f],
                               jnp.pad(wcd_t[c_f:c_a], ((0, F_PAD - FOX_HEADS), (0, 0)))], axis=0)
    cd_wt_s = jnp.concatenate([cd_wt_p, wcd_t[c_q:c_k]], axis=0)
    bf_col = fox_b_f.reshape(FOX_HEADS, 1)
    cb = cf_conv_b.reshape(1, MIX_HALF)
    cg_ = cf_ln_g.reshape(1, MIX_HALF)
    cbb = cf_ln_b.reshape(1, MIX_HALF)
    cd_wo = cd_w_out.astype(BF16)
    ffn_wi = ffn_w_in.astype(BF16)
    ffn_wo = ffn_w_out.astype(BF16)

    def tail(x2d, h1, c1, h2, c2, wo, layer, tm, dec=None):
        return _tail(h1, c1, h2, c2, wo, x2d, ln_g[layer, 0:1], ln_b[layer, 0:1], ffn_wi, ffn_wo, layer,
                     ln_g[layer, 1:2], ln_b[layer, 1:2], tm, dec)

    xs = x_sample.reshape(nd, d)
    pn_s, kc_s, rc_s, qc_s = _proj(xs, ab_wn, ab_wt_s,
                                   ((0, hk, F32), (hk, GLA_RANK, F32), (hk + GLA_RANK, hk, F32)), 1, nd,
                                   as_columns=True)
    mixed_s, conv_a_s, gla_s = _smix0(pn_s.reshape(nd, 1, -1), kc_s, qc_s, rc_s,
                                      state_conv_a, state_gla, ab_conv_w, w2t, bg2c, ng)
    mixed_s = mixed_s.reshape(nd, d)
    xs2 = tail(xs, mixed_s, 0, mixed_s, 1, ab_wo, 0, nd)
    kvf = ((0, MIX_HALF, F32), (MIX_HALF, MIX_HALF, F32), (2 * MIX_HALF, F_PAD, F32))
    cols_s = ((0, MIX_HALF, F32), (MIX_HALF, MIX_HALF, F32), (2 * MIX_HALF, FOX_HEADS, F32),
              (2 * MIX_HALF + F_PAD, MIX_HALF, F32))
    pn1_s, kc1_s, vc1_s, fc1_s, qc1_s = _proj(xs2, cd_wn_s, cd_wt_s, cols_s, 1, nd, as_columns=True)
    caches = (jnp.transpose(cache_k, (0, 2, 3, 1)), jnp.transpose(cache_v, (0, 2, 3, 1)),
              jnp.transpose(cache_logf, (0, 2, 1)))
    dec_small = (qc1_s, kc1_s, vc1_s, fc1_s, bf_col)

    m = nb * seq
    tm = _tile_rows(m, TAIL_ROWS)
    tl = _tile_rows(seq, SEQ_TILE)
    n_pages = page_table.shape[1]
    per_tail = 0
    if n_pages % PAGES_PER_STEP == 0 and (m // tm) % (n_pages // PAGES_PER_STEP) == 0:
        per_tail = (m // tm) // (n_pages // PAGES_PER_STEP)
        if DEPTH * per_tail > nd:
            per_tail = 0
    in_tails = DEPTH * per_tail
    flash_steps = nb * (seq // tl) * (seq // tl + 1) // 2
    in_flash = 0
    if n_pages % FLASH_PAGES == 0 and flash_steps % (n_pages // FLASH_PAGES) == 0:
        in_flash = max(0, min(flash_steps // (n_pages // FLASH_PAGES), nd - in_tails))
    dec_parts = {}

    def prompt_tail(x2d, h1, c1, h2, c2, wo, layer):
        if per_tail == 0:
            return tail(x2d, h1, c1, h2, c2, wo, layer, tm)
        y, o_part, lf_part = tail(x2d, h1, c1, h2, c2, wo, layer, tm,
                                  (page_table, dec_small, caches, layer * per_tail, per_tail))
        dec_parts[layer * per_tail] = (o_part, lf_part)
        return y

    x0 = x_prompt.reshape(m, d)
    pn, kt, rt = _proj(x0, ab_wn, ab_wt_p, ((0, hk, F32), (hk, GLA_RANK, F32)), nb, tl)
    mixed, conv_a_p, gla_p = _mix0(pn, kt, rt, ab_conv_w, w2, w2t, bg2, bg2c, ng, nb)
    x2 = prompt_tail(x0, mixed, 0, mixed, 1, ab_wo, 0)

    pn1, y_d, conv_d_p, kt1, vt1, ft1, kt16, vt16 = _proj_conf(
        x2, cd_wn, cd_wt_p, kvf + ((0, MIX_HALF, BF16), (MIX_HALF, MIX_HALF, BF16)),
        cf_conv_w, cb, cg_, cbb, nb, tl)
    lft, caug = _fox_c(ft1, bf_col)
    if in_flash:
        o_c, o_part, lf_part = _fox_flash(pn1, kt16, vt16, caug, nb, tl,
                                          (page_table, dec_small, caches, in_tails, in_flash, FLASH_PAGES))
        dec_parts[in_tails] = (o_part, lf_part)
    else:
        o_c = _fox_flash(pn1, kt16, vt16, caug, nb, tl)
    y_p = prompt_tail(x2, o_c, 0, y_d, 0, cd_wo, 1)

    y_prompt = y_p.reshape(nb, seq, d)
    k_p = jnp.transpose(kt1.reshape(nb, FOX_HEADS, FOX_HD, seq), (0, 3, 1, 2))
    v_p = jnp.transpose(vt1.reshape(nb, FOX_HEADS, FOX_HD, seq), (0, 3, 1, 2))
    lf_p = jnp.transpose(lft, (0, 2, 1))

    done = in_tails + in_flash
    if done < nd:
        dec_parts[done] = tuple(_decode(page_table, dec_small, caches, done, nd - done))
    o_col = jnp.concatenate([dec_parts[k][0] for k in sorted(dec_parts)], axis=0)
    lf_col = jnp.concatenate([dec_parts[k][1] for k in sorted(dec_parts)], axis=0)
    o_cs = o_col.reshape(nd, MIX_HALF).astype(BF16)
    y_ds, conv_d_s = _sconf(pn1_s.reshape(nd, 1, -1), state_conv_d, cf_conv_w, cb, cg_, cbb)
    y_s = tail(xs2, o_cs, 0, y_ds.reshape(nd, MIX_HALF), 0, cd_wo, 1, nd)

    y_sample = y_s.reshape(nd, 1, d)
    k_s = pn1_s[:, 3 * MIX_HALF:4 * MIX_HALF].reshape(nd, 1, FOX_HEADS, FOX_HD)
    v_s = pn1_s[:, 4 * MIX_HALF:5 * MIX_HALF].reshape(nd, 1, FOX_HEADS, FOX_HD)
    lf_s = lf_col.reshape(nd, 1, FOX_HEADS)

    return (y_prompt, y_sample, conv_a_p, conv_a_s, gla_p, gla_s, k_p, k_s, v_p, v_s, lf_p, lf_s,
            conv_d_p, conv_d_s)
```
